```python
import math
import jax, jax.numpy as jnp
from jax import lax
import numpy as np

D_MODEL = 1024
BATCH = 4
SEQ = 8192
DEPTH = 2

PLE_DIM = 256
N_ATTN_HEADS = 4
ATTN_HALF_DIM = 64
ATTN_V_DIM = 2 * ATTN_HALF_DIM
ATTN_WIDTH = N_ATTN_HEADS * 2 * ATTN_HALF_DIM
N_RWKV_HEADS = 8
RWKV_HEAD_DIM = 64
RWKV_WIDTH = N_RWKV_HEADS * RWKV_HEAD_DIM
DECAY_LORA = 64
AAA_LORA = 64
GATE_LORA = 128
RWKV_COLS = 3 * RWKV_WIDTH + DECAY_LORA + AAA_LORA + GATE_LORA
N_BRANCHES = 2
IN_COLS = 3 * ATTN_WIDTH + RWKV_COLS + N_BRANCHES * D_MODEL
D_FF = -(-8 * D_MODEL // (3 * 256)) * 256
NUM_BUCKETS = 32
MAX_DISTANCE = 128
Q_BLOCK = 128
NORM_EPS = 1e-6
SUBLN_EPS = 1e-5
GN_EPS = 64e-5

kernel_name = "hybrid_diffattn_rwkv7_gated_block"


def rmsnorm(x, g, eps=NORM_EPS):
    xf = x.astype(jnp.float32)
    y = xf * lax.rsqrt(jnp.mean(xf * xf, axis=-1, keepdims=True) + eps)
    return (y * g.astype(jnp.float32)).astype(x.dtype)


def token_shift(t):
    return jnp.pad(t, ((0, 0), (1, 0), (0, 0)))[:, :-1]


def t5_bucket(dist):
    n = jnp.maximum(dist, 0)
    max_exact = NUM_BUCKETS // 2
    nf = jnp.maximum(n, max_exact).astype(jnp.float32)
    large = max_exact + (jnp.log(nf / max_exact) / math.log(MAX_DISTANCE / max_exact)
                         * (NUM_BUCKETS - max_exact)).astype(jnp.int32)
    large = jnp.minimum(large, NUM_BUCKETS - 1)
    return jnp.where(n < max_exact, n, large)


def diff_attention(q, k, v, rel_bias, lam):
    B, S, H, _, Dh = q.shape
    nb = S // Q_BLOCK
    scale = Dh ** -0.5
    kt = k.transpose(0, 2, 3, 1, 4)
    vt = v.transpose(0, 2, 1, 3)
    qb = (q * scale).reshape(B, nb, Q_BLOCK, H, 2, Dh).transpose(1, 0, 3, 4, 2, 5)
    k_pos = jnp.arange(S)
    lam32 = lam.astype(jnp.float32)

    def block(args):
        q_blk, start = args
        s = jnp.einsum('bhcqd,bhckd->bhcqk', q_blk, kt).astype(jnp.float32)
        dist = (start + jnp.arange(Q_BLOCK))[:, None] - k_pos[None, :]
        bias = rel_bias[t5_bucket(dist)].astype(jnp.float32)
        s = s + bias.transpose(2, 3, 0, 1)[None]
        s = jnp.where(dist >= 0, s, -jnp.inf)
        pr = jax.nn.softmax(s, axis=-1)
        wts = pr[:, :, 0] - lam32 * pr[:, :, 1]
        return jnp.einsum('bhqk,bhkd->bhqd', wts.astype(vt.dtype), vt)

    starts = jnp.arange(nb) * Q_BLOCK
    out = lax.map(block, (qb, starts))
    return out.transpose(1, 0, 3, 2, 4).reshape(B, S, H, v.shape[-1])


def rwkv7_scan(r, w, k, v, a, b):
    dt = r.dtype
    B, S, H, N = r.shape
    to_t = lambda t: t.astype(jnp.float32).transpose(1, 0, 2, 3)

    def step(state, inp):
        r_t, w_t, k_t, v_t, a_t, b_t = inp
        sa = jnp.einsum('bhvk,bhk->bhv', state, a_t)
        state = (state * w_t[:, :, None, :] + sa[..., None] * b_t[:, :, None, :]
                 + v_t[..., None] * k_t[:, :, None, :])
        return state, jnp.einsum('bhvk,bhk->bhv', state, r_t)

    s0 = jnp.zeros((B, H, N, N), jnp.float32)
    _, y = lax.scan(step, s0, (to_t(r), to_t(w), to_t(k), to_t(v), to_t(a), to_t(b)))
    return y.transpose(1, 0, 2, 3).astype(dt)


def setup_inputs(seed: int = 0) -> dict:
    key = jax.random.key(seed)
    ks = iter(jax.random.split(key, 40))
    nrm = lambda shape, scale: jax.random.normal(next(ks), shape, jnp.float32) * scale
    gain = lambda shape: 1.0 + nrm(shape, 0.02)
    L, D = DEPTH, D_MODEL
    return {
        "x": nrm((BATCH, SEQ, D), 1.0),
        "p": nrm((DEPTH, BATCH, SEQ, PLE_DIM), 1.0),
        "rel_bias": nrm((NUM_BUCKETS, N_ATTN_HEADS, 2), 0.5),
        "norm_mix": gain((L, D)),
        "w_in": nrm((L, D, IN_COLS), D ** -0.5),
        "lam_q1": nrm((L, ATTN_HALF_DIM), 0.1),
        "lam_k1": nrm((L, ATTN_HALF_DIM), 0.1),
        "lam_q2": nrm((L, ATTN_HALF_DIM), 0.1),
        "lam_k2": nrm((L, ATTN_HALF_DIM), 0.1),
        "attn_subln": gain((L, ATTN_V_DIM)),
        "rwkv_mu": jax.random.uniform(next(ks), (L, RWKV_COLS), jnp.float32, 0.0, 1.0),
        "rwkv_w0": jax.random.uniform(next(ks), (L, RWKV_WIDTH), jnp.float32, -6.0, -1.0),
        "rwkv_w2": nrm((L, DECAY_LORA, RWKV_WIDTH), 0.1 * DECAY_LORA ** -0.5),
        "rwkv_a0": nrm((L, RWKV_WIDTH), 0.5),
        "rwkv_a2": nrm((L, AAA_LORA, RWKV_WIDTH), 0.5 * AAA_LORA ** -0.5),
        "rwkv_g2": nrm((L, GATE_LORA, RWKV_WIDTH), GATE_LORA ** -0.5),
        "rwkv_kk": 0.85 + nrm((L, RWKV_WIDTH), 0.05),
        "rwkv_ka": 1.0 + nrm((L, RWKV_WIDTH), 0.05),
        "rwkv_rk": nrm((L, N_RWKV_HEADS, RWKV_HEAD_DIM), 0.1),
        "rwkv_lnx_w": gain((L, RWKV_WIDTH)),
        "rwkv_lnx_b": nrm((L, RWKV_WIDTH), 0.02),
        "w_out_attn": nrm((L, ATTN_WIDTH, D), ATTN_WIDTH ** -0.5),
        "w_out_rwkv": nrm((L, RWKV_WIDTH, D), RWKV_WIDTH ** -0.5),
        "w_out": nrm((L, D, D), D ** -0.5),
        "norm_ffn": gain((L, D)),
        "w_ffn_gate": nrm((L, D, D_FF), D ** -0.5),
        "w_ffn_up": nrm((L, D, D_FF), D ** -0.5),
        "w_ffn_down": nrm((L, D_FF, D), D_FF ** -0.5),
        "norm_ple": gain((L, D)),
        "w_ple": nrm((L, PLE_DIM, D), PLE_DIM ** -0.5),
        "w_ple_gate": nrm((L, D, D), D ** -0.5),
        "norm_final": gain((D,)),
    }


def reference(x, p, rel_bias, norm_mix, w_in, lam_q1, lam_k1, lam_q2, lam_k2, attn_subln,
              rwkv_mu, rwkv_w0, rwkv_w2, rwkv_a0, rwkv_a2, rwkv_g2, rwkv_kk, rwkv_ka, rwkv_rk,
              rwkv_lnx_w, rwkv_lnx_b, w_out_attn, w_out_rwkv, w_out, norm_ffn, w_ffn_gate,
              w_ffn_up, w_ffn_down, norm_ple, w_ple, w_ple_gate, norm_final):
    B, S, D = x.shape
    H, Dh, Dv = N_ATTN_HEADS, ATTN_HALF_DIM, ATTN_V_DIM
    NH, N = N_RWKV_HEADS, RWKV_HEAD_DIM
    heads = lambda t: t.reshape(B, S, NH, N)
    for i in range(DEPTH):
        h = rmsnorm(x, norm_mix[i])
        z = h @ w_in[i]
        o = 0
        qa = z[..., o:o + ATTN_WIDTH].reshape(B, S, H, 2, Dh); o += ATTN_WIDTH
        ka = z[..., o:o + ATTN_WIDTH].reshape(B, S, H, 2, Dh); o += ATTN_WIDTH
        va = z[..., o:o + ATTN_WIDTH].reshape(B, S, H, Dv); o += ATTN_WIDTH
        zr = z[..., o:o + RWKV_COLS]; o += RWKV_COLS
        gate_a = jax.nn.sigmoid(z[..., o:o + D]); o += D
        gate_r = jax.nn.sigmoid(z[..., o:o + D])

        lam_init = 0.8 - 0.6 * math.exp(-0.3 * i)
        lam = (jnp.exp(jnp.sum(lam_q1[i] * lam_k1[i])) - jnp.exp(jnp.sum(lam_q2[i] * lam_k2[i]))
               + lam_init)
        oa = diff_attention(qa, ka, va, rel_bias, lam)
        oa = rmsnorm(oa, attn_subln[i], SUBLN_EPS) * (1.0 - lam_init)
        y_attn = oa.reshape(B, S, ATTN_WIDTH) @ w_out_attn[i]

        zs = zr + (token_shift(zr) - zr) * rwkv_mu[i]
        c = 0
        r = zs[..., c:c + RWKV_WIDTH]; c += RWKV_WIDTH
        kr = zs[..., c:c + RWKV_WIDTH]; c += RWKV_WIDTH
        vr = zs[..., c:c + RWKV_WIDTH]; c += RWKV_WIDTH
        xw = zs[..., c:c + DECAY_LORA]; c += DECAY_LORA
        xa = zs[..., c:c + AAA_LORA]; c += AAA_LORA
        xg = zs[..., c:c + GATE_LORA]
        wlog = -jax.nn.softplus(-(rwkv_w0[i] + jnp.tanh(xw) @ rwkv_w2[i])) - 0.5
        decay = jnp.exp(-jnp.exp(wlog.astype(jnp.float32)))
        a = jax.nn.sigmoid(rwkv_a0[i] + xa @ rwkv_a2[i])
        g = jax.nn.sigmoid(xg) @ rwkv_g2[i]
        kk = heads(kr * rwkv_kk[i]).astype(jnp.float32)
        kk = kk / jnp.maximum(jnp.sqrt(jnp.sum(kk * kk, axis=-1, keepdims=True)), 1e-12)
        kr = kr * (1.0 + (a - 1.0) * rwkv_ka[i])
        rh, kh, vh, ah = heads(r), heads(kr), heads(vr), heads(a).astype(jnp.float32)
        yr = rwkv7_scan(rh, heads(decay), kh, vh, -kk, kk * ah)
        yf = yr.astype(jnp.float32)
        mu = jnp.mean(yf, axis=-1, keepdims=True)
        var = jnp.mean(jnp.square(yf - mu), axis=-1, keepdims=True)
        yn = ((yf - mu) * lax.rsqrt(var + GN_EPS)).reshape(B, S, RWKV_WIDTH)
        yn = (yn * rwkv_lnx_w[i].astype(jnp.float32) + rwkv_lnx_b[i].astype(jnp.float32)).astype(x.dtype)
        bonus = jnp.sum(rh * kh * rwkv_rk[i], axis=-1, keepdims=True) * vh
        y_rwkv = ((yn + bonus.reshape(B, S, RWKV_WIDTH)) * g) @ w_out_rwkv[i]

        x = x + (gate_a * y_attn + gate_r * y_rwkv) @ w_out[i]

        h2 = rmsnorm(x, norm_ffn[i])
        x = x + (jax.nn.silu(h2 @ w_ffn_gate[i]) * (h2 @ w_ffn_up[i])) @ w_ffn_down[i]

        e = p[i] @ w_ple[i]
        gp = jax.nn.sigmoid(rmsnorm(x, norm_ple[i]) @ w_ple_gate[i])
        x = x + gp * e
    return rmsnorm(x, norm_final)
```

```python
import functools
import math

import jax
import jax.numpy as jnp
from jax import lax
from jax.experimental import pallas as pl
from jax.experimental.pallas import tpu as pltpu

F32 = jnp.float32
BF16 = jnp.bfloat16

N_ATTN_HEADS = 4
ATTN_HALF_DIM = 64
ATTN_V_DIM = 128
ATTN_WIDTH = 512
N_RWKV_HEADS = 8
RWKV_HEAD_DIM = 64
RWKV_WIDTH = 512
DECAY_LORA = 64
AAA_LORA = 64
GATE_LORA = 128
RWKV_COLS = 3 * RWKV_WIDTH + DECAY_LORA + AAA_LORA + GATE_LORA
NUM_BUCKETS = 32
MAX_DISTANCE = 128
NORM_EPS = 1e-6
SUBLN_EPS = 1e-5
GN_EPS = 64e-5

LANES = 128
VMEM_LIMIT_BYTES = 56 * 1024 * 1024
ATTN_TILE = 256
SCAN_CHUNK = 64
MASK_VALUE = -1e30


def _cparams(*sem):
    return pltpu.CompilerParams(dimension_semantics=sem, vmem_limit_bytes=VMEM_LIMIT_BYTES)


def _const_spec(shape):
    nd = len(shape)
    return pl.BlockSpec(shape, lambda *_: (0,) * nd, pipeline_mode=pl.Buffered(1))


def _dot(a, b):
    return jnp.dot(a.astype(BF16), b.astype(BF16), preferred_element_type=F32)


def _dot_nt(a, b):
    return lax.dot_general(a.astype(BF16), b.astype(BF16), (((1,), (1,)), ((), ())),
                           preferred_element_type=F32)


def _dot_tn(a, b):
    return lax.dot_general(a.astype(BF16), b.astype(BF16), (((0,), (0,)), ((), ())),
                           preferred_element_type=F32)


def _dot_hilo(a_exact_bf16, x):
    hi = x.astype(BF16)
    lo = (x - hi.astype(F32)).astype(BF16)
    return (jnp.dot(a_exact_bf16, hi, preferred_element_type=F32)
            + jnp.dot(a_exact_bf16, lo, preferred_element_type=F32))


def _rms(x, g, eps):
    return x * lax.rsqrt(jnp.mean(x * x, axis=-1, keepdims=True) + eps) * g


def _inproj_kernel(x_ref, g_ref, w_ref, qkv_ref, zr_ref, gates_ref):
    h = _rms(x_ref[...], g_ref[...], NORM_EPS).astype(BF16)
    scale = ATTN_HALF_DIM ** -0.5
    nq = 3 * ATTN_WIDTH
    for c0 in range(0, nq, 512):
        z = jnp.dot(h, w_ref[:, c0:c0 + 512], preferred_element_type=F32)
        if c0 < ATTN_WIDTH:
            z = z * scale
        qkv_ref[:, c0:c0 + 512] = z.astype(BF16)
    c0 = nq
    while c0 < nq + RWKV_COLS:
        w = min(512, nq + RWKV_COLS - c0)
        zr_ref[:, c0 - nq:c0 - nq + w] = jnp.dot(h, w_ref[:, c0:c0 + w], preferred_element_type=F32)
        c0 += w
    base = nq + RWKV_COLS
    for c0 in range(0, gates_ref.shape[1], 512):
        z = jnp.dot(h, w_ref[:, base + c0:base + c0 + 512], preferred_element_type=F32)
        gates_ref[:, c0:c0 + 512] = jax.nn.sigmoid(z).astype(BF16)


def _inproj(x2d, g, w_bf16, tm):
    T, D = x2d.shape
    ncols = w_bf16.shape[1]
    ngate = ncols - 3 * ATTN_WIDTH - RWKV_COLS
    return pl.pallas_call(
        _inproj_kernel,
        grid=(T // tm,),
        in_specs=[pl.BlockSpec((tm, D), lambda i: (i, 0)),
                  _const_spec((1, D)),
                  _const_spec((D, ncols))],
        out_specs=[pl.BlockSpec((tm, 3 * ATTN_WIDTH), lambda i: (i, 0)),
                   pl.BlockSpec((tm, RWKV_COLS), lambda i: (i, 0)),
                   pl.BlockSpec((tm, ngate), lambda i: (i, 0))],
        out_shape=[jax.ShapeDtypeStruct((T, 3 * ATTN_WIDTH), BF16),
                   jax.ShapeDtypeStruct((T, RWKV_COLS), F32),
                   jax.ShapeDtypeStruct((T, ngate), BF16)],
        compiler_params=_cparams("parallel"),
        name="inproj",
    )(x2d, g.reshape(1, D), w_bf16)


def _bias_tiles_kernel(rb_ref, out_ref):
    hc = pl.program_id(0)
    t = out_ref.shape[-1]
    rows = lax.broadcasted_iota(jnp.int32, (t, t), 0)
    cols = lax.broadcasted_iota(jnp.int32, (t, t), 1)
    max_exact = NUM_BUCKETS // 2
    n_hc = 2 * N_ATTN_HEADS
    far = rb_ref[(NUM_BUCKETS - 1) * n_hc + hc]
    for kind in range(2):
        dist = kind * t + rows - cols
        n = jnp.maximum(dist, 0)
        nf = jnp.maximum(n, max_exact).astype(F32)
        large = max_exact + (jnp.log(nf / max_exact) / math.log(MAX_DISTANCE / max_exact)
                             * (NUM_BUCKETS - max_exact)).astype(jnp.int32)
        large = jnp.minimum(large, NUM_BUCKETS - 1)
        bucket = jnp.where(n < max_exact, n, large)
        tile = jnp.zeros((t, t), F32)
        for b in range(NUM_BUCKETS):
            tile = jnp.where(bucket == b, rb_ref[b * n_hc + hc], tile)
        tile = tile - far
        if kind == 0:
            tile = jnp.where(dist >= 0, tile, MASK_VALUE)
        out_ref[0, kind] = tile


def _bias_tiles(rel_bias, t):
    n_hc = 2 * N_ATTN_HEADS
    return pl.pallas_call(
        _bias_tiles_kernel,
        grid=(n_hc,),
        in_specs=[pl.BlockSpec(memory_space=pltpu.SMEM)],
        out_specs=pl.BlockSpec((1, 2, t, t), lambda i: (i, 0, 0, 0)),
        out_shape=jax.ShapeDtypeStruct((n_hc, 2, t, t), F32),
        compiler_params=_cparams("parallel"),
        name="bias_tiles",
    )(rel_bias.reshape(-1).astype(F32))


def _attn_kernel(scal_ref, q_ref, k_ref, v_ref, bias_ref, g_ref, o_ref, m_ref, l_ref, acc_ref):
    t = q_ref.shape[1]
    qi = pl.program_id(2)
    lane = lax.broadcasted_iota(jnp.int32, (1, LANES), 1)
    q = q_ref[0]
    zero = jnp.zeros_like(q)
    qm = (jnp.where(lane < ATTN_HALF_DIM, q, zero), jnp.where(lane >= ATTN_HALF_DIM, q, zero))

    m_ref[...] = jnp.full(m_ref.shape, MASK_VALUE, F32)
    l_ref[...] = jnp.zeros(l_ref.shape, F32)
    acc_ref[...] = jnp.zeros(acc_ref.shape, F32)

    def step(kj, kind):
        start = pl.multiple_of(kj * t, t)
        kb = k_ref[0, pl.ds(start, t), :]
        vb = v_ref[0, pl.ds(start, t), :]
        for c in range(2):
            s = lax.dot_general(qm[c], kb, (((1,), (1,)), ((), ())), preferred_element_type=F32)
            if kind is not None:
                s = s + bias_ref[0, c, kind]
            m_prev = m_ref[c]
            m_new = jnp.maximum(m_prev, jnp.max(s, axis=-1, keepdims=True))
            alpha = jnp.exp(m_prev - m_new)
            p = jnp.exp(s - m_new)
            l_ref[c] = alpha * l_ref[c] + jnp.sum(p, axis=-1, keepdims=True)
            acc_ref[c] = alpha * acc_ref[c] + jnp.dot(p.astype(BF16), vb, preferred_element_type=F32)
            m_ref[c] = m_new

    def far_body(kj, carry):
        step(kj, None)
        return carry

    lax.fori_loop(0, jnp.maximum(qi - 1, 0), far_body, 0)

    @pl.when(qi >= 1)
    def _():
        step(qi - 1, 1)

    step(qi, 0)

    lam = scal_ref[0]
    out_scale = scal_ref[1]
    o = acc_ref[0] / l_ref[0] - lam * (acc_ref[1] / l_ref[1])
    o_ref[0] = (_rms(o, g_ref[...], SUBLN_EPS) * out_scale).astype(o_ref.dtype)


def _attention(qkv, bias_tiles, scal, subln_g, t):
    B, S, _ = qkv.shape
    H = N_ATTN_HEADS
    return pl.pallas_call(
        _attn_kernel,
        grid=(B, H, S // t),
        in_specs=[pl.BlockSpec(memory_space=pltpu.SMEM),
                  pl.BlockSpec((1, t, LANES), lambda b, h, i: (b, i, h)),
                  pl.BlockSpec((1, S, LANES), lambda b, h, i: (b, 0, H + h)),
                  pl.BlockSpec((1, S, LANES), lambda b, h, i: (b, 0, 2 * H + h)),
                  pl.BlockSpec((1, 2, 2, t, t), lambda b, h, i: (h, 0, 0, 0, 0)),
                  _const_spec((1, ATTN_V_DIM))],
        out_specs=pl.BlockSpec((1, t, LANES), lambda b, h, i: (b, i, h)),
        out_shape=jax.ShapeDtypeStruct((B, S, ATTN_WIDTH), BF16),
        scratch_shapes=[pltpu.VMEM((2, t, 1), F32), pltpu.VMEM((2, t, 1), F32),
                        pltpu.VMEM((2, t, ATTN_V_DIM), F32)],
        compiler_params=_cparams("parallel", "parallel", "arbitrary"),
        name="diff_attention",
    )(scal, qkv, qkv, qkv, bias_tiles.reshape(H, 2, 2, t, t), subln_g.reshape(1, ATTN_V_DIM))


def _rwkv_prep_kernel(z_ref, zp_ref, mu_ref, w0_ref, w2_ref, a0_ref, a2_ref, g2_ref, kk_ref, ka_ref,
                      rk_ref, seg_ref,
                      r_ref, lw_ref, k_ref, v_ref, a_ref, b_ref, g_ref, bg_ref):
    i = pl.program_id(1)
    z = z_ref[0]
    tm = z.shape[0]
    W = RWKV_WIDTH
    prev_row = zp_ref[0, 7:8, :] * (i > 0).astype(F32)
    row = lax.broadcasted_iota(jnp.int32, (tm, 1), 0)
    prev = jnp.where(row == 0, prev_row, pltpu.roll(z, 1, 0))
    zs = z + (prev - z) * mu_ref[...]
    r = zs[:, 0:W]
    kr = zs[:, W:2 * W]
    vr = zs[:, 2 * W:3 * W]
    xwa = zs[:, 3 * W:3 * W + DECAY_LORA + AAA_LORA]
    xg = zs[:, 3 * W + DECAY_LORA + AAA_LORA:]

    dw = w0_ref[...] + _dot(jnp.tanh(xwa), w2_ref[...])
    softplus = jnp.maximum(-dw, 0.0) + jnp.log1p(jnp.exp(-jnp.abs(dw)))
    lw_ref[0] = -jnp.exp(-softplus - 0.5)
    asig = jax.nn.sigmoid(a0_ref[...] + _dot(xwa, a2_ref[...]))
    g = _dot(jax.nn.sigmoid(xg), g2_ref[...])

    seg = seg_ref[...]
    kk = kr * kk_ref[...]
    norm = jnp.sqrt(_dot_hilo_rhs_exact(kk * kk, seg))
    kkn = kk / jnp.maximum(norm, 1e-12)
    kmod = kr * (1.0 + (asig - 1.0) * ka_ref[...])
    bonus = _dot_hilo_rhs_exact(r * kmod * rk_ref[...], seg) * vr

    r_ref[0] = r
    k_ref[0] = kmod
    v_ref[0] = vr
    a_ref[0] = -kkn
    b_ref[0] = kkn * asig
    g_ref[0] = g
    bg_ref[0] = bonus * g


def _dot_hilo_rhs_exact(x, seg):
    hi = x.astype(BF16)
    lo = (x - hi.astype(F32)).astype(BF16)
    return (jnp.dot(hi, seg, preferred_element_type=F32) + jnp.dot(lo, seg, preferred_element_type=F32))


def _rwkv_prep(zr, mu, w0, w2, a0, a2, g2, kkp, ka, rk, tm):
    B, S, C = zr.shape
    W = RWKV_WIDTH
    w2p = jnp.concatenate([w2, jnp.zeros_like(w2)], axis=0).astype(BF16)
    a2p = jnp.concatenate([jnp.zeros_like(a2), a2], axis=0).astype(BF16)
    head = jnp.arange(W) // RWKV_HEAD_DIM
    seg = (head[:, None] == head[None, :]).astype(BF16)
    row = lambda v: v.reshape(1, -1).astype(F32)
    nblk8 = tm // 8
    out_spec = pl.BlockSpec((1, tm, W), lambda b, i: (b, i, 0))
    return pl.pallas_call(
        _rwkv_prep_kernel,
        grid=(B, S // tm),
        in_specs=[pl.BlockSpec((1, tm, C), lambda b, i: (b, i, 0)),
                  pl.BlockSpec((1, 8, C), lambda b, i: (b, jnp.maximum(i * nblk8 - 1, 0), 0)),
                  _const_spec((1, C)), _const_spec((1, W)), _const_spec((LANES, W)),
                  _const_spec((1, W)), _const_spec((LANES, W)), _const_spec((GATE_LORA, W)),
                  _const_spec((1, W)), _const_spec((1, W)), _const_spec((1, W)),
                  _const_spec((W, W))],
        out_specs=[out_spec] * 8,
        out_shape=[jax.ShapeDtypeStruct((B, S, W), F32)] * 8,
        compiler_params=_cparams("parallel", "parallel"),
        name="rwkv_prep",
    )(zr, zr, row(mu), row(w0), w2p, row(a0), a2p, g2.astype(BF16), row(kkp), row(ka), row(rk), seg)


def _scan_chunk_local(r, lw, k, v, a, b, consts):
    C = SCAN_CHUNK
    tril, strict2, incl2, lane_lo, lane_hi, bd = consts
    L = _dot_hilo(tril, lw)
    wt = jnp.exp(L)
    winv = jnp.exp(-L)
    wprev = jnp.exp(L - lw)
    l_end = L[C - 1:C, :]
    wc = jnp.exp(l_end)
    wend = jnp.exp(l_end - L)
    rt = r * wt
    at = a * wprev
    bt = b * winv
    kt = k * winv
    bk = jnp.concatenate([bt, kt], axis=0)
    ar = jnp.concatenate([at, rt], axis=0)
    zero = jnp.zeros_like(ar)
    lhs = jnp.concatenate([jnp.where(lane_lo, ar, zero), jnp.where(lane_hi, ar, zero)], axis=0)
    out = _dot_nt(lhs, bk)
    top0 = jnp.where(strict2, out[0:C], 0.0)
    bot0 = jnp.where(incl2, out[C:2 * C], 0.0)
    top1 = jnp.where(strict2, out[2 * C:3 * C], 0.0)
    bot1 = jnp.where(incl2, out[3 * C:4 * C], 0.0)

    zc = jnp.zeros((C, LANES), F32)
    vv = jnp.concatenate([v, v], axis=0)
    aak_v = _dot(jnp.concatenate([jnp.where(lane_hi, top0, zc), jnp.where(lane_hi, top1, zc)], axis=0), vv)
    A = jnp.concatenate([jnp.where(lane_lo, top0, zc),
                         jnp.where(lane_hi, pltpu.roll(top1, C, 1), zc)], axis=0)
    Z = jnp.concatenate([jnp.concatenate([at, at], axis=0), aak_v], axis=1)
    n_steps = int(math.log2(C))
    for s in range(n_steps):
        Z = Z + _dot(A, Z)
        if s + 1 < n_steps:
            A = _dot(A, A)
    ahat = jnp.where(lane_lo, Z[0:C, 0:LANES], Z[C:2 * C, 0:LANES])
    vhat = jnp.where(lane_lo, Z[0:C, LANES:], Z[C:2 * C, LANES:])

    rhs = jnp.concatenate([jnp.concatenate([vhat, ahat], axis=1),
                           jnp.concatenate([v, zc], axis=1)], axis=0)
    yr2 = _dot(jnp.concatenate([bot0, bot1], axis=0), rhs)
    yhat = jnp.where(lane_lo, yr2[0:C, 0:LANES], yr2[C:2 * C, 0:LANES])
    rhat = rt + jnp.where(lane_lo, yr2[0:C, LANES:], yr2[C:2 * C, LANES:])

    x = jnp.concatenate([jnp.concatenate([ahat, vhat], axis=1),
                         jnp.concatenate([zc, v], axis=1)], axis=0)
    bk_end = jnp.concatenate([b * wend, k * wend], axis=0)
    qn = _dot_tn(x, bk_end)
    Q = jnp.where(bd, qn[0:LANES], 0.0)
    Nt = jnp.where(bd, qn[LANES:], 0.0)
    return rhat, yhat, Q, Nt, wc


def _rwkv_scan_kernel(r_ref, lw_ref, k_ref, v_ref, a_ref, b_ref, g_ref, bg_ref, lnw_ref, lnb_ref,
                      o_ref, s_ref):
    C = SCAN_CHUNK
    n_chunks = r_ref.shape[1] // C

    @pl.when(pl.program_id(2) == 0)
    def _():
        s_ref[...] = jnp.zeros(s_ref.shape, F32)

    ri = lax.broadcasted_iota(jnp.int32, (C, 2 * C), 0)
    ci = lax.broadcasted_iota(jnp.int32, (C, 2 * C), 1) % C
    strict2 = ci < ri
    incl2 = ci <= ri
    lane = lax.broadcasted_iota(jnp.int32, (1, LANES), 1)
    lane_lo = lane < RWKV_HEAD_DIM
    lane_hi = lane >= RWKV_HEAD_DIM
    r128 = lax.broadcasted_iota(jnp.int32, (LANES, LANES), 0)
    c128 = lax.broadcasted_iota(jnp.int32, (LANES, LANES), 1)
    bd = (r128 < RWKV_HEAD_DIM) == (c128 < RWKV_HEAD_DIM)
    tril = (lax.broadcasted_iota(jnp.int32, (C, C), 1) <= lax.broadcasted_iota(jnp.int32, (C, C), 0)).astype(BF16)
    seg_mean = jnp.where(bd, 1.0, 0.0).astype(BF16)
    consts = (tril, strict2, incl2, lane_lo, lane_hi, bd)

    local = []
    for c in range(n_chunks):
        sl = slice(c * C, (c + 1) * C)
        local.append(_scan_chunk_local(r_ref[0, sl, :], lw_ref[0, sl, :], k_ref[0, sl, :], v_ref[0, sl, :],
                                       a_ref[0, sl, :], b_ref[0, sl, :], consts))

    S = s_ref[...]
    inv_n = 1.0 / RWKV_HEAD_DIM
    for c in range(n_chunks):
        sl = slice(c * C, (c + 1) * C)
        rhat, yhat, Q, Nt, wc = local[c]
        y = _dot_nt(rhat, S) + yhat
        S = S * wc + _dot(S, Q) + Nt
        mean = _dot_hilo_rhs_exact(y, seg_mean) * inv_n
        d = y - mean
        var = _dot_hilo_rhs_exact(d * d, seg_mean) * inv_n
        yn = d * lax.rsqrt(var + GN_EPS) * lnw_ref[...] + lnb_ref[...]
        o_ref[0, sl, :] = (yn * g_ref[0, sl, :] + bg_ref[0, sl, :]).astype(o_ref.dtype)
    s_ref[...] = S


def _rwkv_scan(r, lw, k, v, a, b, g, bg, lnw, lnb, tm):
    B, S, W = r.shape
    n_pairs = W // LANES
    spec = pl.BlockSpec((1, tm, LANES), lambda bb, p, i: (bb, i, p))
    vec = pl.BlockSpec((1, LANES), lambda bb, p, i: (0, p))
    return pl.pallas_call(
        _rwkv_scan_kernel,
        grid=(B, n_pairs, S // tm),
        in_specs=[spec] * 8 + [vec, vec],
        out_specs=spec,
        out_shape=jax.ShapeDtypeStruct((B, S, W), BF16),
        scratch_shapes=[pltpu.VMEM((LANES, LANES), F32)],
        compiler_params=_cparams("parallel", "parallel", "arbitrary"),
        name="rwkv_scan",
    )(r, lw, k, v, a, b, g, bg, lnw.reshape(1, W).astype(F32), lnb.reshape(1, W).astype(F32))


def _post_kernel(x_ref, oa_ref, yr_ref, gates_ref, p_ref, woa_ref, wor_ref, wout_ref, nffn_ref,
                 wg_ref, wu_ref, wd_ref, nple_ref, wple_ref, wpg_ref, nfin_ref, o_ref, *, final, ff_chunk):
    D = x_ref.shape[1]
    ya = jnp.dot(oa_ref[...], woa_ref[...], preferred_element_type=F32)
    yr = jnp.dot(yr_ref[...], wor_ref[...], preferred_element_type=F32)
    m = gates_ref[:, 0:D].astype(F32) * ya + gates_ref[:, D:2 * D].astype(F32) * yr
    x = x_ref[...] + _dot(m, wout_ref[...])

    h2 = _rms(x, nffn_ref[...], NORM_EPS).astype(BF16)
    d_ff = wg_ref.shape[1]
    acc = jnp.zeros_like(x)
    for c0 in range(0, d_ff, ff_chunk):
        gt = jnp.dot(h2, wg_ref[:, c0:c0 + ff_chunk], preferred_element_type=F32)
        up = jnp.dot(h2, wu_ref[:, c0:c0 + ff_chunk], preferred_element_type=F32)
        act = gt * jax.nn.sigmoid(gt) * up
        acc = acc + jnp.dot(act.astype(BF16), wd_ref[c0:c0 + ff_chunk, :], preferred_element_type=F32)
    x = x + acc

    e = _dot(p_ref[...], wple_ref[...])
    gp = jax.nn.sigmoid(_dot(_rms(x, nple_ref[...], NORM_EPS), wpg_ref[...]))
    x = x + gp * e
    if final:
        x = _rms(x, nfin_ref[...], NORM_EPS)
    o_ref[...] = x


def _post(x2d, oa, yr, gates, p2d, woa, wor, wout, nffn, wg, wu, wd, nple, wple, wpg, nfin, final, tm):
    T, D = x2d.shape
    d_ff = wg.shape[1]
    ff_chunk = 256 if d_ff % 256 == 0 else LANES
    tok = lambda w: pl.BlockSpec((tm, w), lambda i: (i, 0))
    row = lambda v: v.reshape(1, -1).astype(F32)
    bf = lambda w: w.astype(BF16)
    return pl.pallas_call(
        functools.partial(_post_kernel, final=final, ff_chunk=ff_chunk),
        grid=(T // tm,),
        in_specs=[tok(D), tok(oa.shape[1]), tok(yr.shape[1]), tok(gates.shape[1]), tok(p2d.shape[1]),
                  _const_spec(woa.shape), _const_spec(wor.shape), _const_spec(wout.shape),
                  _const_spec((1, D)), _const_spec(wg.shape), _const_spec(wu.shape), _const_spec(wd.shape),
                  _const_spec((1, D)), _const_spec(wple.shape), _const_spec(wpg.shape), _const_spec((1, D))],
        out_specs=tok(D),
        out_shape=jax.ShapeDtypeStruct((T, D), F32),
        compiler_params=_cparams("parallel"),
        name="post",
    )(x2d, oa, yr, gates, p2d, bf(woa), bf(wor), bf(wout), row(nffn), bf(wg), bf(wu), bf(wd),
      row(nple), bf(wple), bf(wpg), row(nfin))


def _tile(n, pref):
    t = min(n, pref)
    assert n % t == 0, (n, t)
    return t


def kernel(x, p, rel_bias, norm_mix, w_in, lam_q1, lam_k1, lam_q2, lam_k2, attn_subln, rwkv_mu, rwkv_w0, rwkv_w2, rwkv_a0, rwkv_a2, rwkv_g2, rwkv_kk, rwkv_ka, rwkv_rk, rwkv_lnx_w, rwkv_lnx_b, w_out_attn, w_out_rwkv, w_out, norm_ffn, w_ffn_gate, w_ffn_up, w_ffn_down, norm_ple, w_ple, w_ple_gate, norm_final):
    B, S, D = x.shape
    depth = w_in.shape[0]
    T = B * S
    t_attn = _tile(S, ATTN_TILE)
    assert t_attn >= 113
    tm_tok = _tile(T, 512)
    tm_prep = _tile(S, 256)
    tm_scan = _tile(S, 4 * SCAN_CHUNK)
    assert tm_scan % SCAN_CHUNK == 0

    bias_tiles = _bias_tiles(rel_bias, t_attn)
    x2d = x.reshape(T, D)
    for i in range(depth):
        qkv, zr, gates = _inproj(x2d, norm_mix[i], w_in[i].astype(BF16), tm_tok)

        lam_init = 0.8 - 0.6 * math.exp(-0.3 * i)
        lam = (jnp.exp(jnp.sum(lam_q1[i] * lam_k1[i])) - jnp.exp(jnp.sum(lam_q2[i] * lam_k2[i])) + lam_init)
        scal = jnp.stack([lam.astype(F32), jnp.asarray(1.0 - lam_init, F32)])
        oa = _attention(qkv.reshape(B, S, -1), bias_tiles, scal, attn_subln[i], t_attn)

        r, lw, k, v, a, b, g, bg = _rwkv_prep(zr.reshape(B, S, -1), rwkv_mu[i], rwkv_w0[i], rwkv_w2[i],
                                              rwkv_a0[i], rwkv_a2[i], rwkv_g2[i], rwkv_kk[i], rwkv_ka[i],
                                              rwkv_rk[i].reshape(-1), tm_prep)
        yr = _rwkv_scan(r, lw, k, v, a, b, g, bg, rwkv_lnx_w[i], rwkv_lnx_b[i], tm_scan)

        x2d = _post(x2d, oa.reshape(T, -1), yr.reshape(T, -1), gates, p[i].reshape(T, -1),
                    w_out_attn[i], w_out_rwkv[i], w_out[i], norm_ffn[i], w_ffn_gate[i], w_ffn_up[i],
                    w_ffn_down[i], norm_ple[i], w_ple[i], w_ple_gate[i], norm_final,
                    final=(i == depth - 1), tm=tm_tok)
    return x2d.reshape(B, S, D)
```

```python
import functools
import math

import jax
import jax.numpy as jnp
from jax import lax
from jax.experimental import pallas as pl
from jax.experimental.pallas import tpu as pltpu

F32 = jnp.float32
BF16 = jnp.bfloat16

N_ATTN_HEADS = 4
ATTN_HALF_DIM = 64
ATTN_V_DIM = 128
ATTN_WIDTH = 512
N_RWKV_HEADS = 8
RWKV_HEAD_DIM = 64
RWKV_WIDTH = 512
DECAY_LORA = 64
AAA_LORA = 64
GATE_LORA = 128
RWKV_COLS = 3 * RWKV_WIDTH + DECAY_LORA + AAA_LORA + GATE_LORA
NUM_BUCKETS = 32
MAX_DISTANCE = 128
NORM_EPS = 1e-6
SUBLN_EPS = 1e-5
GN_EPS = 64e-5

LANES = 128
VMEM_LIMIT_BYTES = 56 * 1024 * 1024
ATTN_TILE = 512
ATTN_FAR_TILES = 2
LOG2_E = math.log2(math.e)
SCAN_CHUNK = 64
MASK_VALUE = -1e30


def _cparams(*sem):
    return pltpu.CompilerParams(dimension_semantics=sem, vmem_limit_bytes=VMEM_LIMIT_BYTES)


def _const_spec(shape):
    nd = len(shape)
    return pl.BlockSpec(shape, lambda *_: (0,) * nd, pipeline_mode=pl.Buffered(1))


def _dot(a, b):
    return jnp.dot(a.astype(BF16), b.astype(BF16), preferred_element_type=F32)


def _dot_nt(a, b):
    return lax.dot_general(a.astype(BF16), b.astype(BF16), (((1,), (1,)), ((), ())),
                           preferred_element_type=F32)


def _dot_tn(a, b):
    return lax.dot_general(a.astype(BF16), b.astype(BF16), (((0,), (0,)), ((), ())),
                           preferred_element_type=F32)


def _dot_hilo(a_exact_bf16, x):
    hi = x.astype(BF16)
    lo = (x - hi.astype(F32)).astype(BF16)
    return (jnp.dot(a_exact_bf16, hi, preferred_element_type=F32)
            + jnp.dot(a_exact_bf16, lo, preferred_element_type=F32))


def _rms(x, g, eps):
    return x * lax.rsqrt(jnp.mean(x * x, axis=-1, keepdims=True) + eps) * g


def _inproj_kernel(x_ref, g_ref, w_ref, qkv_ref, zr_ref, gates_ref):
    h = _rms(x_ref[...], g_ref[...], NORM_EPS).astype(BF16)
    scale = ATTN_HALF_DIM ** -0.5 * LOG2_E
    nq = 3 * ATTN_WIDTH
    for c0 in range(0, nq, 512):
        z = jnp.dot(h, w_ref[:, c0:c0 + 512], preferred_element_type=F32)
        if c0 < ATTN_WIDTH:
            z = z * scale
        qkv_ref[:, c0:c0 + 512] = z.astype(BF16)
    c0 = nq
    while c0 < nq + RWKV_COLS:
        w = min(512, nq + RWKV_COLS - c0)
        zr_ref[:, c0 - nq:c0 - nq + w] = jnp.dot(h, w_ref[:, c0:c0 + w], preferred_element_type=F32)
        c0 += w
    base = nq + RWKV_COLS
    for c0 in range(0, gates_ref.shape[1], 512):
        z = jnp.dot(h, w_ref[:, base + c0:base + c0 + 512], preferred_element_type=F32)
        gates_ref[:, c0:c0 + 512] = jax.nn.sigmoid(z).astype(BF16)


def _inproj(x2d, g, w_bf16, tm):
    T, D = x2d.shape
    ncols = w_bf16.shape[1]
    ngate = ncols - 3 * ATTN_WIDTH - RWKV_COLS
    return pl.pallas_call(
        _inproj_kernel,
        grid=(T // tm,),
        in_specs=[pl.BlockSpec((tm, D), lambda i: (i, 0)),
                  _const_spec((1, D)),
                  _const_spec((D, ncols))],
        out_specs=[pl.BlockSpec((tm, 3 * ATTN_WIDTH), lambda i: (i, 0)),
                   pl.BlockSpec((tm, RWKV_COLS), lambda i: (i, 0)),
                   pl.BlockSpec((tm, ngate), lambda i: (i, 0))],
        out_shape=[jax.ShapeDtypeStruct((T, 3 * ATTN_WIDTH), BF16),
                   jax.ShapeDtypeStruct((T, RWKV_COLS), F32),
                   jax.ShapeDtypeStruct((T, ngate), BF16)],
        compiler_params=_cparams("parallel"),
        name="inproj",
    )(x2d, g.reshape(1, D), w_bf16)


def _bias_tiles_kernel(rb_ref, out_ref):
    hc = pl.program_id(0)
    t = out_ref.shape[-1]
    rows = lax.broadcasted_iota(jnp.int32, (t, t), 0)
    cols = lax.broadcasted_iota(jnp.int32, (t, t), 1)
    max_exact = NUM_BUCKETS // 2
    n_hc = 2 * N_ATTN_HEADS
    far = rb_ref[(NUM_BUCKETS - 1) * n_hc + hc]
    for kind in range(2):
        dist = kind * t + rows - cols
        n = jnp.maximum(dist, 0)
        nf = jnp.maximum(n, max_exact).astype(F32)
        large = max_exact + (jnp.log(nf / max_exact) / math.log(MAX_DISTANCE / max_exact)
                             * (NUM_BUCKETS - max_exact)).astype(jnp.int32)
        large = jnp.minimum(large, NUM_BUCKETS - 1)
        bucket = jnp.where(n < max_exact, n, large)
        tile = jnp.zeros((t, t), F32)
        for b in range(NUM_BUCKETS):
            tile = jnp.where(bucket == b, rb_ref[b * n_hc + hc], tile)
        tile = (tile - far) * LOG2_E
        if kind == 0:
            tile = jnp.where(dist >= 0, tile, MASK_VALUE)
        out_ref[0, kind] = tile


def _bias_tiles(rel_bias, t):
    n_hc = 2 * N_ATTN_HEADS
    return pl.pallas_call(
        _bias_tiles_kernel,
        grid=(n_hc,),
        in_specs=[pl.BlockSpec(memory_space=pltpu.SMEM)],
        out_specs=pl.BlockSpec((1, 2, t, t), lambda i: (i, 0, 0, 0)),
        out_shape=jax.ShapeDtypeStruct((n_hc, 2, t, t), F32),
        compiler_params=_cparams("parallel"),
        name="bias_tiles",
    )(rel_bias.reshape(-1).astype(F32))


def _attn_kernel(scal_ref, q_ref, k_ref, v_ref, bias_ref, g_ref, o_ref, m_ref, l_ref, acc_ref, *, far_tiles):
    t = q_ref.shape[1]
    qi = pl.program_id(2)
    lane = lax.broadcasted_iota(jnp.int32, (1, LANES), 1)
    q = q_ref[0]
    zero = jnp.zeros_like(q)
    qm = (jnp.where(lane < ATTN_HALF_DIM, q, zero), jnp.where(lane >= ATTN_HALF_DIM, q, zero))

    m_ref[...] = jnp.full(m_ref.shape, MASK_VALUE, F32)
    l_ref[...] = jnp.zeros(l_ref.shape, F32)
    acc_ref[...] = jnp.zeros(acc_ref.shape, F32)

    def step(start, width, kind):
        kb = k_ref[0, pl.ds(start, width), :]
        vb = v_ref[0, pl.ds(start, width), :]
        for c in range(2):
            s = lax.dot_general(qm[c], kb, (((1,), (1,)), ((), ())), preferred_element_type=F32)
            if kind is not None:
                s = s + bias_ref[0, c, kind]
            m_prev = m_ref[c]
            m_new = jnp.maximum(m_prev, jnp.max(s, axis=-1, keepdims=True))
            alpha = jnp.exp2(m_prev - m_new)
            p = jnp.exp2(s - pltpu.repeat(m_new, width // LANES, axis=1))
            l_ref[c] = alpha * l_ref[c] + jnp.sum(p, axis=-1, keepdims=True)
            acc_ref[c] = alpha * acc_ref[c] + jnp.dot(p.astype(BF16), vb, preferred_element_type=F32)
            m_ref[c] = m_new

    n_far = jnp.maximum(qi - 1, 0)
    far_w = far_tiles * t

    def far_body(j, carry):
        step(pl.multiple_of(j * far_w, far_w), far_w, None)
        return carry

    n_wide = n_far // far_tiles
    lax.fori_loop(0, n_wide, far_body, 0)
    for extra in range(far_tiles - 1):
        @pl.when(n_far - n_wide * far_tiles > extra)
        def _():
            step(pl.multiple_of((n_wide * far_tiles + extra) * t, t), t, None)

    @pl.when(qi >= 1)
    def _():
        step(pl.multiple_of((qi - 1) * t, t), t, 1)

    step(pl.multiple_of(qi * t, t), t, 0)

    lam = scal_ref[0]
    out_scale = scal_ref[1]
    o = acc_ref[0] / l_ref[0] - lam * (acc_ref[1] / l_ref[1])
    o_ref[0] = (_rms(o, g_ref[...], SUBLN_EPS) * out_scale).astype(o_ref.dtype)


def _attention(qkv, bias_tiles, scal, subln_g, t, far_tiles):
    B, S, _ = qkv.shape
    H = N_ATTN_HEADS
    return pl.pallas_call(
        functools.partial(_attn_kernel, far_tiles=far_tiles),
        grid=(B, H, S // t),
        in_specs=[pl.BlockSpec(memory_space=pltpu.SMEM),
                  pl.BlockSpec((1, t, LANES), lambda b, h, i: (b, i, h)),
                  pl.BlockSpec((1, S, LANES), lambda b, h, i: (b, 0, H + h)),
                  pl.BlockSpec((1, S, LANES), lambda b, h, i: (b, 0, 2 * H + h)),
                  pl.BlockSpec((1, 2, 2, t, t), lambda b, h, i: (h, 0, 0, 0, 0)),
                  _const_spec((1, ATTN_V_DIM))],
        out_specs=pl.BlockSpec((1, t, LANES), lambda b, h, i: (b, i, h)),
        out_shape=jax.ShapeDtypeStruct((B, S, ATTN_WIDTH), BF16),
        scratch_shapes=[pltpu.VMEM((2, t, LANES), F32), pltpu.VMEM((2, t, LANES), F32),
                        pltpu.VMEM((2, t, ATTN_V_DIM), F32)],
        compiler_params=_cparams("parallel", "parallel", "arbitrary"),
        name="diff_attention",
    )(scal, qkv, qkv, qkv, bias_tiles.reshape(H, 2, 2, t, t), subln_g.reshape(1, ATTN_V_DIM))


def _rwkv_prep_kernel(z_ref, zp_ref, mu_ref, w0_ref, w2_ref, a0_ref, a2_ref, g2_ref, kk_ref, ka_ref,
                      rk_ref, seg_ref,
                      r_ref, lw_ref, k_ref, v_ref, a_ref, b_ref, g_ref, bg_ref):
    i = pl.program_id(1)
    z = z_ref[0]
    tm = z.shape[0]
    W = RWKV_WIDTH
    prev_row = zp_ref[0, 7:8, :] * (i > 0).astype(F32)
    row = lax.broadcasted_iota(jnp.int32, (tm, 1), 0)
    prev = jnp.where(row == 0, prev_row, pltpu.roll(z, 1, 0))
    zs = z + (prev - z) * mu_ref[...]
    r = zs[:, 0:W]
    kr = zs[:, W:2 * W]
    vr = zs[:, 2 * W:3 * W]
    xwa = zs[:, 3 * W:3 * W + DECAY_LORA + AAA_LORA]
    xg = zs[:, 3 * W + DECAY_LORA + AAA_LORA:]

    dw = w0_ref[...] + _dot(jnp.tanh(xwa), w2_ref[...])
    softplus = jnp.maximum(-dw, 0.0) + jnp.log1p(jnp.exp(-jnp.abs(dw)))
    lw_ref[0] = -jnp.exp(-softplus - 0.5)
    asig = jax.nn.sigmoid(a0_ref[...] + _dot(xwa, a2_ref[...]))
    g = _dot(jax.nn.sigmoid(xg), g2_ref[...])

    seg = seg_ref[...]
    kk = kr * kk_ref[...]
    norm = jnp.sqrt(_dot_hilo_rhs_exact(kk * kk, seg))
    kkn = kk / jnp.maximum(norm, 1e-12)
    kmod = kr * (1.0 + (asig - 1.0) * ka_ref[...])
    bonus = _dot_hilo_rhs_exact(r * kmod * rk_ref[...], seg) * vr

    r_ref[0] = r
    k_ref[0] = kmod
    v_ref[0] = vr
    a_ref[0] = -kkn
    b_ref[0] = kkn * asig
    g_ref[0] = g
    bg_ref[0] = bonus * g


def _dot_hilo_rhs_exact(x, seg):
    hi = x.astype(BF16)
    lo = (x - hi.astype(F32)).astype(BF16)
    return (jnp.dot(hi, seg, preferred_element_type=F32) + jnp.dot(lo, seg, preferred_element_type=F32))


def _rwkv_prep(zr, mu, w0, w2, a0, a2, g2, kkp, ka, rk, tm):
    B, S, C = zr.shape
    W = RWKV_WIDTH
    w2p = jnp.concatenate([w2, jnp.zeros_like(w2)], axis=0).astype(BF16)
    a2p = jnp.concatenate([jnp.zeros_like(a2), a2], axis=0).astype(BF16)
    head = jnp.arange(W) // RWKV_HEAD_DIM
    seg = (head[:, None] == head[None, :]).astype(BF16)
    row = lambda v: v.reshape(1, -1).astype(F32)
    nblk8 = tm // 8
    out_spec = pl.BlockSpec((1, tm, W), lambda b, i: (b, i, 0))
    return pl.pallas_call(
        _rwkv_prep_kernel,
        grid=(B, S // tm),
        in_specs=[pl.BlockSpec((1, tm, C), lambda b, i: (b, i, 0)),
                  pl.BlockSpec((1, 8, C), lambda b, i: (b, jnp.maximum(i * nblk8 - 1, 0), 0)),
                  _const_spec((1, C)), _const_spec((1, W)), _const_spec((LANES, W)),
                  _const_spec((1, W)), _const_spec((LANES, W)), _const_spec((GATE_LORA, W)),
                  _const_spec((1, W)), _const_spec((1, W)), _const_spec((1, W)),
                  _const_spec((W, W))],
        out_specs=[out_spec] * 8,
        out_shape=[jax.ShapeDtypeStruct((B, S, W), F32)] * 8,
        compiler_params=_cparams("parallel", "parallel"),
        name="rwkv_prep",
    )(zr, zr, row(mu), row(w0), w2p, row(a0), a2p, g2.astype(BF16), row(kkp), row(ka), row(rk), seg)


def _scan_chunk_local(r, lw, k, v, a, b, consts):
    C = SCAN_CHUNK
    tril, strict2, incl2, lane_lo, lane_hi, bd = consts
    L = _dot_hilo(tril, lw)
    wt = jnp.exp(L)
    winv = jnp.exp(-L)
    wprev = jnp.exp(L - lw)
    l_end = L[C - 1:C, :]
    wc = jnp.exp(l_end)
    wend = jnp.exp(l_end - L)
    rt = r * wt
    at = a * wprev
    bt = b * winv
    kt = k * winv
    bk = jnp.concatenate([bt, kt], axis=0)
    ar = jnp.concatenate([at, rt], axis=0)
    zero = jnp.zeros_like(ar)
    lhs = jnp.concatenate([jnp.where(lane_lo, ar, zero), jnp.where(lane_hi, ar, zero)], axis=0)
    out = _dot_nt(lhs, bk)
    top0 = jnp.where(strict2, out[0:C], 0.0)
    bot0 = jnp.where(incl2, out[C:2 * C], 0.0)
    top1 = jnp.where(strict2, out[2 * C:3 * C], 0.0)
    bot1 = jnp.where(incl2, out[3 * C:4 * C], 0.0)

    zc = jnp.zeros((C, LANES), F32)
    vv = jnp.concatenate([v, v], axis=0)
    aak_v = _dot(jnp.concatenate([jnp.where(lane_hi, top0, zc), jnp.where(lane_hi, top1, zc)], axis=0), vv)
    A = jnp.concatenate([jnp.where(lane_lo, top0, zc),
                         jnp.where(lane_hi, pltpu.roll(top1, C, 1), zc)], axis=0)
    Z = jnp.concatenate([jnp.concatenate([at, at], axis=0), aak_v], axis=1)
    n_steps = int(math.log2(C))
    for s in range(n_steps):
        Z = Z + _dot(A, Z)
        if s + 1 < n_steps:
            A = _dot(A, A)
    ahat = jnp.where(lane_lo, Z[0:C, 0:LANES], Z[C:2 * C, 0:LANES])
    vhat = jnp.where(lane_lo, Z[0:C, LANES:], Z[C:2 * C, LANES:])

    rhs = jnp.concatenate([jnp.concatenate([vhat, ahat], axis=1),
                           jnp.concatenate([v, zc], axis=1)], axis=0)
    yr2 = _dot(jnp.concatenate([bot0, bot1], axis=0), rhs)
    yhat = jnp.where(lane_lo, yr2[0:C, 0:LANES], yr2[C:2 * C, 0:LANES])
    rhat = rt + jnp.where(lane_lo, yr2[0:C, LANES:], yr2[C:2 * C, LANES:])

    x = jnp.concatenate([jnp.concatenate([ahat, vhat], axis=1),
                         jnp.concatenate([zc, v], axis=1)], axis=0)
    bk_end = jnp.concatenate([b * wend, k * wend], axis=0)
    qn = _dot_tn(x, bk_end)
    Q = jnp.where(bd, qn[0:LANES], 0.0)
    Nt = jnp.where(bd, qn[LANES:], 0.0)
    return rhat, yhat, Q, Nt, wc


def _rwkv_scan_kernel(r_ref, lw_ref, k_ref, v_ref, a_ref, b_ref, g_ref, bg_ref, lnw_ref, lnb_ref,
                      o_ref, s_ref):
    C = SCAN_CHUNK
    n_chunks = r_ref.shape[1] // C

    @pl.when(pl.program_id(2) == 0)
    def _():
        s_ref[...] = jnp.zeros(s_ref.shape, F32)

    ri = lax.broadcasted_iota(jnp.int32, (C, 2 * C), 0)
    ci = lax.broadcasted_iota(jnp.int32, (C, 2 * C), 1) % C
    strict2 = ci < ri
    incl2 = ci <= ri
    lane = lax.broadcasted_iota(jnp.int32, (1, LANES), 1)
    lane_lo = lane < RWKV_HEAD_DIM
    lane_hi = lane >= RWKV_HEAD_DIM
    r128 = lax.broadcasted_iota(jnp.int32, (LANES, LANES), 0)
    c128 = lax.broadcasted_iota(jnp.int32, (LANES, LANES), 1)
    bd = (r128 < RWKV_HEAD_DIM) == (c128 < RWKV_HEAD_DIM)
    tril = (lax.broadcasted_iota(jnp.int32, (C, C), 1) <= lax.broadcasted_iota(jnp.int32, (C, C), 0)).astype(BF16)
    seg_mean = jnp.where(bd, 1.0, 0.0).astype(BF16)
    consts = (tril, strict2, incl2, lane_lo, lane_hi, bd)

    local = []
    for c in range(n_chunks):
        sl = slice(c * C, (c + 1) * C)
        local.append(_scan_chunk_local(r_ref[0, sl, :], lw_ref[0, sl, :], k_ref[0, sl, :], v_ref[0, sl, :],
                                       a_ref[0, sl, :], b_ref[0, sl, :], consts))

    S = s_ref[...]
    inv_n = 1.0 / RWKV_HEAD_DIM
    for c in range(n_chunks):
        sl = slice(c * C, (c + 1) * C)
        rhat, yhat, Q, Nt, wc = local[c]
        y = _dot_nt(rhat, S) + yhat
        S = S * wc + _dot(S, Q) + Nt
        mean = _dot_hilo_rhs_exact(y, seg_mean) * inv_n
        d = y - mean
        var = _dot_hilo_rhs_exact(d * d, seg_mean) * inv_n
        yn = d * lax.rsqrt(var + GN_EPS) * lnw_ref[...] + lnb_ref[...]
        o_ref[0, sl, :] = (yn * g_ref[0, sl, :] + bg_ref[0, sl, :]).astype(o_ref.dtype)
    s_ref[...] = S


def _rwkv_scan(r, lw, k, v, a, b, g, bg, lnw, lnb, tm):
    B, S, W = r.shape
    n_pairs = W // LANES
    spec = pl.BlockSpec((1, tm, LANES), lambda bb, p, i: (bb, i, p))
    vec = pl.BlockSpec((1, LANES), lambda bb, p, i: (0, p))
    return pl.pallas_call(
        _rwkv_scan_kernel,
        grid=(B, n_pairs, S // tm),
        in_specs=[spec] * 8 + [vec, vec],
        out_specs=spec,
        out_shape=jax.ShapeDtypeStruct((B, S, W), BF16),
        scratch_shapes=[pltpu.VMEM((LANES, LANES), F32)],
        compiler_params=_cparams("parallel", "parallel", "arbitrary"),
        name="rwkv_scan",
    )(r, lw, k, v, a, b, g, bg, lnw.reshape(1, W).astype(F32), lnb.reshape(1, W).astype(F32))


def _post_kernel(x_ref, oa_ref, yr_ref, gates_ref, p_ref, woa_ref, wor_ref, wout_ref, nffn_ref,
                 wg_ref, wu_ref, wd_ref, nple_ref, wple_ref, wpg_ref, nfin_ref, o_ref, *, final, ff_chunk):
    D = x_ref.shape[1]
    ya = jnp.dot(oa_ref[...], woa_ref[...], preferred_element_type=F32)
    yr = jnp.dot(yr_ref[...], wor_ref[...], preferred_element_type=F32)
    m = gates_ref[:, 0:D].astype(F32) * ya + gates_ref[:, D:2 * D].astype(F32) * yr
    x = x_ref[...] + _dot(m, wout_ref[...])

    h2 = _rms(x, nffn_ref[...], NORM_EPS).astype(BF16)
    d_ff = wg_ref.shape[1]
    acc = jnp.zeros_like(x)
    for c0 in range(0, d_ff, ff_chunk):
        gt = jnp.dot(h2, wg_ref[:, c0:c0 + ff_chunk], preferred_element_type=F32)
        up = jnp.dot(h2, wu_ref[:, c0:c0 + ff_chunk], preferred_element_type=F32)
        act = gt * jax.nn.sigmoid(gt) * up
        acc = acc + jnp.dot(act.astype(BF16), wd_ref[c0:c0 + ff_chunk, :], preferred_element_type=F32)
    x = x + acc

    e = _dot(p_ref[...], wple_ref[...])
    gp = jax.nn.sigmoid(_dot(_rms(x, nple_ref[...], NORM_EPS), wpg_ref[...]))
    x = x + gp * e
    if final:
        x = _rms(x, nfin_ref[...], NORM_EPS)
    o_ref[...] = x


def _post(x2d, oa, yr, gates, p2d, woa, wor, wout, nffn, wg, wu, wd, nple, wple, wpg, nfin, final, tm):
    T, D = x2d.shape
    d_ff = wg.shape[1]
    ff_chunk = 256 if d_ff % 256 == 0 else LANES
    tok = lambda w: pl.BlockSpec((tm, w), lambda i: (i, 0))
    row = lambda v: v.reshape(1, -1).astype(F32)
    bf = lambda w: w.astype(BF16)
    return pl.pallas_call(
        functools.partial(_post_kernel, final=final, ff_chunk=ff_chunk),
        grid=(T // tm,),
        in_specs=[tok(D), tok(oa.shape[1]), tok(yr.shape[1]), tok(gates.shape[1]), tok(p2d.shape[1]),
                  _const_spec(woa.shape), _const_spec(wor.shape), _const_spec(wout.shape),
                  _const_spec((1, D)), _const_spec(wg.shape), _const_spec(wu.shape), _const_spec(wd.shape),
                  _const_spec((1, D)), _const_spec(wple.shape), _const_spec(wpg.shape), _const_spec((1, D))],
        out_specs=tok(D),
        out_shape=jax.ShapeDtypeStruct((T, D), F32),
        compiler_params=_cparams("parallel"),
        name="post",
    )(x2d, oa, yr, gates, p2d, bf(woa), bf(wor), bf(wout), row(nffn), bf(wg), bf(wu), bf(wd),
      row(nple), bf(wple), bf(wpg), row(nfin))


def _tile(n, pref):
    t = min(n, pref)
    assert n % t == 0, (n, t)
    return t


def kernel(x, p, rel_bias, norm_mix, w_in, lam_q1, lam_k1, lam_q2, lam_k2, attn_subln, rwkv_mu, rwkv_w0, rwkv_w2, rwkv_a0, rwkv_a2, rwkv_g2, rwkv_kk, rwkv_ka, rwkv_rk, rwkv_lnx_w, rwkv_lnx_b, w_out_attn, w_out_rwkv, w_out, norm_ffn, w_ffn_gate, w_ffn_up, w_ffn_down, norm_ple, w_ple, w_ple_gate, norm_final):
    B, S, D = x.shape
    depth = w_in.shape[0]
    T = B * S
    t_attn = _tile(S, ATTN_TILE)
    assert t_attn >= 113
    tm_tok = _tile(T, 512)
    tm_prep = _tile(S, 256)
    tm_scan = _tile(S, 4 * SCAN_CHUNK)
    assert tm_scan % SCAN_CHUNK == 0

    bias_tiles = _bias_tiles(rel_bias, t_attn)
    x2d = x.reshape(T, D)
    for i in range(depth):
        qkv, zr, gates = _inproj(x2d, norm_mix[i], w_in[i].astype(BF16), tm_tok)

        lam_init = 0.8 - 0.6 * math.exp(-0.3 * i)
        lam = (jnp.exp(jnp.sum(lam_q1[i] * lam_k1[i])) - jnp.exp(jnp.sum(lam_q2[i] * lam_k2[i])) + lam_init)
        scal = jnp.stack([lam.astype(F32), jnp.asarray(1.0 - lam_init, F32)])
        oa = _attention(qkv.reshape(B, S, -1), bias_tiles, scal, attn_subln[i], t_attn, ATTN_FAR_TILES)

        r, lw, k, v, a, b, g, bg = _rwkv_prep(zr.reshape(B, S, -1), rwkv_mu[i], rwkv_w0[i], rwkv_w2[i],
                                              rwkv_a0[i], rwkv_a2[i], rwkv_g2[i], rwkv_kk[i], rwkv_ka[i],
                                              rwkv_rk[i].reshape(-1), tm_prep)
        yr = _rwkv_scan(r, lw, k, v, a, b, g, bg, rwkv_lnx_w[i], rwkv_lnx_b[i], tm_scan)

        x2d = _post(x2d, oa.reshape(T, -1), yr.reshape(T, -1), gates, p[i].reshape(T, -1),
                    w_out_attn[i], w_out_rwkv[i], w_out[i], norm_ffn[i], w_ffn_gate[i], w_ffn_up[i],
                    w_ffn_down[i], norm_ple[i], w_ple[i], w_ple_gate[i], norm_final,
                    final=(i == depth - 1), tm=tm_tok)
    return x2d.reshape(B, S, D)
```

```python
import functools
import math

import jax
import jax.numpy as jnp
from jax import lax
from jax.experimental import pallas as pl
from jax.experimental.pallas import tpu as pltpu

F32 = jnp.float32
BF16 = jnp.bfloat16

N_ATTN_HEADS = 4
ATTN_HALF_DIM = 64
ATTN_V_DIM = 128
ATTN_WIDTH = 512
N_RWKV_HEADS = 8
RWKV_HEAD_DIM = 64
RWKV_WIDTH = 512
DECAY_LORA = 64
AAA_LORA = 64
GATE_LORA = 128
RWKV_COLS = 3 * RWKV_WIDTH + DECAY_LORA + AAA_LORA + GATE_LORA
NUM_BUCKETS = 32
MAX_DISTANCE = 128
NORM_EPS = 1e-6
SUBLN_EPS = 1e-5
GN_EPS = 64e-5

LANES = 128
VMEM_LIMIT_BYTES = 56 * 1024 * 1024
ATTN_TILE = 512
ATTN_FAR_TILES = 2
LOG2_E = math.log2(math.e)
SCAN_CHUNK = 64
MASK_VALUE = -1e30


def _cparams(*sem):
    return pltpu.CompilerParams(dimension_semantics=sem, vmem_limit_bytes=VMEM_LIMIT_BYTES)


def _const_spec(shape):
    nd = len(shape)
    return pl.BlockSpec(shape, lambda *_: (0,) * nd, pipeline_mode=pl.Buffered(1))


def _dot(a, b):
    return jnp.dot(a.astype(BF16), b.astype(BF16), preferred_element_type=F32)


def _dot_nt(a, b):
    return lax.dot_general(a.astype(BF16), b.astype(BF16), (((1,), (1,)), ((), ())),
                           preferred_element_type=F32)


def _dot_tn(a, b):
    return lax.dot_general(a.astype(BF16), b.astype(BF16), (((0,), (0,)), ((), ())),
                           preferred_element_type=F32)


def _dot_hilo(a_exact_bf16, x):
    hi = x.astype(BF16)
    lo = (x - hi.astype(F32)).astype(BF16)
    return (jnp.dot(a_exact_bf16, hi, preferred_element_type=F32)
            + jnp.dot(a_exact_bf16, lo, preferred_element_type=F32))


def _rms(x, g, eps):
    return x * lax.rsqrt(jnp.mean(x * x, axis=-1, keepdims=True) + eps) * g


def _inproj_kernel(x_ref, g_ref, w_ref, qkv_ref, zr_ref, gates_ref):
    h = _rms(x_ref[...], g_ref[...], NORM_EPS).astype(BF16)
    scale = ATTN_HALF_DIM ** -0.5 * LOG2_E
    nq = 3 * ATTN_WIDTH
    for c0 in range(0, nq, 512):
        z = jnp.dot(h, w_ref[:, c0:c0 + 512], preferred_element_type=F32)
        if c0 < ATTN_WIDTH:
            z = z * scale
        qkv_ref[:, c0:c0 + 512] = z.astype(BF16)
    c0 = nq
    while c0 < nq + RWKV_COLS:
        w = min(512, nq + RWKV_COLS - c0)
        zr_ref[:, c0 - nq:c0 - nq + w] = jnp.dot(h, w_ref[:, c0:c0 + w], preferred_element_type=F32)
        c0 += w
    base = nq + RWKV_COLS
    for c0 in range(0, gates_ref.shape[1], 512):
        z = jnp.dot(h, w_ref[:, base + c0:base + c0 + 512], preferred_element_type=F32)
        gates_ref[:, c0:c0 + 512] = jax.nn.sigmoid(z).astype(BF16)


def _inproj(x2d, g, w_bf16, tm):
    T, D = x2d.shape
    ncols = w_bf16.shape[1]
    ngate = ncols - 3 * ATTN_WIDTH - RWKV_COLS
    return pl.pallas_call(
        _inproj_kernel,
        grid=(T // tm,),
        in_specs=[pl.BlockSpec((tm, D), lambda i: (i, 0)),
                  _const_spec((1, D)),
                  _const_spec((D, ncols))],
        out_specs=[pl.BlockSpec((tm, 3 * ATTN_WIDTH), lambda i: (i, 0)),
                   pl.BlockSpec((tm, RWKV_COLS), lambda i: (i, 0)),
                   pl.BlockSpec((tm, ngate), lambda i: (i, 0))],
        out_shape=[jax.ShapeDtypeStruct((T, 3 * ATTN_WIDTH), BF16),
                   jax.ShapeDtypeStruct((T, RWKV_COLS), F32),
                   jax.ShapeDtypeStruct((T, ngate), BF16)],
        compiler_params=_cparams("parallel"),
        name="inproj",
    )(x2d, g.reshape(1, D), w_bf16)


def _bias_tiles_kernel(rb_ref, out_ref):
    hc = pl.program_id(0)
    t = out_ref.shape[-1]
    rows = lax.broadcasted_iota(jnp.int32, (t, t), 0)
    cols = lax.broadcasted_iota(jnp.int32, (t, t), 1)
    max_exact = NUM_BUCKETS // 2
    n_hc = 2 * N_ATTN_HEADS
    far = rb_ref[(NUM_BUCKETS - 1) * n_hc + hc]
    for kind in range(2):
        dist = kind * t + rows - cols
        n = jnp.maximum(dist, 0)
        nf = jnp.maximum(n, max_exact).astype(F32)
        large = max_exact + (jnp.log(nf / max_exact) / math.log(MAX_DISTANCE / max_exact)
                             * (NUM_BUCKETS - max_exact)).astype(jnp.int32)
        large = jnp.minimum(large, NUM_BUCKETS - 1)
        bucket = jnp.where(n < max_exact, n, large)
        tile = jnp.zeros((t, t), F32)
        for b in range(NUM_BUCKETS):
            tile = jnp.where(bucket == b, rb_ref[b * n_hc + hc], tile)
        tile = (tile - far) * LOG2_E
        if kind == 0:
            tile = jnp.where(dist >= 0, tile, MASK_VALUE)
        out_ref[0, kind] = tile


def _bias_tiles(rel_bias, t):
    n_hc = 2 * N_ATTN_HEADS
    return pl.pallas_call(
        _bias_tiles_kernel,
        grid=(n_hc,),
        in_specs=[pl.BlockSpec(memory_space=pltpu.SMEM)],
        out_specs=pl.BlockSpec((1, 2, t, t), lambda i: (i, 0, 0, 0)),
        out_shape=jax.ShapeDtypeStruct((n_hc, 2, t, t), F32),
        compiler_params=_cparams("parallel"),
        name="bias_tiles",
    )(rel_bias.reshape(-1).astype(F32))


def _attn_kernel(scal_ref, q_ref, k_ref, v_ref, bias_ref, g_ref, o_ref, m_ref, l_ref, acc_ref, *, far_tiles):
    t = q_ref.shape[1]
    qi = pl.program_id(2)
    lane = lax.broadcasted_iota(jnp.int32, (1, LANES), 1)
    q = q_ref[0]
    zero = jnp.zeros_like(q)
    qm = (jnp.where(lane < ATTN_HALF_DIM, q, zero), jnp.where(lane >= ATTN_HALF_DIM, q, zero))

    m_ref[...] = jnp.full(m_ref.shape, MASK_VALUE, F32)
    l_ref[...] = jnp.zeros(l_ref.shape, F32)
    acc_ref[...] = jnp.zeros(acc_ref.shape, F32)

    def step(start, width, kind):
        kb = k_ref[0, pl.ds(start, width), :]
        vb = v_ref[0, pl.ds(start, width), :]
        for c in range(2):
            s = lax.dot_general(qm[c], kb, (((1,), (1,)), ((), ())), preferred_element_type=F32)
            if kind is not None:
                s = s + bias_ref[0, c, kind]
            m_prev = m_ref[c]
            m_new = jnp.maximum(m_prev, jnp.max(s, axis=-1, keepdims=True))
            alpha = jnp.exp2(m_prev - m_new)
            p = jnp.exp2(s - jnp.concatenate([m_new] * (width // LANES), axis=1))
            l_ref[c] = alpha * l_ref[c] + jnp.sum(p, axis=-1, keepdims=True)
            acc_ref[c] = alpha * acc_ref[c] + jnp.dot(p.astype(BF16), vb, preferred_element_type=F32)
            m_ref[c] = m_new

    n_far = jnp.maximum(qi - 1, 0)
    far_w = far_tiles * t

    def far_body(j, carry):
        step(pl.multiple_of(j * far_w, far_w), far_w, None)
        return carry

    n_wide = n_far // far_tiles
    lax.fori_loop(0, n_wide, far_body, 0)
    for extra in range(far_tiles - 1):
        @pl.when(n_far - n_wide * far_tiles > extra)
        def _():
            step(pl.multiple_of((n_wide * far_tiles + extra) * t, t), t, None)

    @pl.when(qi >= 1)
    def _():
        step(pl.multiple_of((qi - 1) * t, t), t, 1)

    step(pl.multiple_of(qi * t, t), t, 0)

    lam = scal_ref[0]
    out_scale = scal_ref[1]
    o = acc_ref[0] / l_ref[0] - lam * (acc_ref[1] / l_ref[1])
    o_ref[0] = (_rms(o, g_ref[...], SUBLN_EPS) * out_scale).astype(o_ref.dtype)


def _attention(qkv, bias_tiles, scal, subln_g, t, far_tiles):
    B, S, _ = qkv.shape
    H = N_ATTN_HEADS
    return pl.pallas_call(
        functools.partial(_attn_kernel, far_tiles=far_tiles),
        grid=(B, H, S // t),
        in_specs=[pl.BlockSpec(memory_space=pltpu.SMEM),
                  pl.BlockSpec((1, t, LANES), lambda b, h, i: (b, i, h)),
                  pl.BlockSpec((1, S, LANES), lambda b, h, i: (b, 0, H + h)),
                  pl.BlockSpec((1, S, LANES), lambda b, h, i: (b, 0, 2 * H + h)),
                  pl.BlockSpec((1, 2, 2, t, t), lambda b, h, i: (h, 0, 0, 0, 0)),
                  _const_spec((1, ATTN_V_DIM))],
        out_specs=pl.BlockSpec((1, t, LANES), lambda b, h, i: (b, i, h)),
        out_shape=jax.ShapeDtypeStruct((B, S, ATTN_WIDTH), BF16),
        scratch_shapes=[pltpu.VMEM((2, t, LANES), F32), pltpu.VMEM((2, t, LANES), F32),
                        pltpu.VMEM((2, t, ATTN_V_DIM), F32)],
        compiler_params=_cparams("parallel", "parallel", "arbitrary"),
        name="diff_attention",
    )(scal, qkv, qkv, qkv, bias_tiles.reshape(H, 2, 2, t, t), subln_g.reshape(1, ATTN_V_DIM))


def _rwkv_prep_kernel(z_ref, zp_ref, mu_ref, w0_ref, w2_ref, a0_ref, a2_ref, g2_ref, kk_ref, ka_ref,
                      rk_ref, seg_ref,
                      r_ref, lw_ref, k_ref, v_ref, a_ref, b_ref, g_ref, bg_ref):
    i = pl.program_id(1)
    z = z_ref[0]
    tm = z.shape[0]
    W = RWKV_WIDTH
    prev_row = zp_ref[0, 7:8, :] * (i > 0).astype(F32)
    row = lax.broadcasted_iota(jnp.int32, (tm, 1), 0)
    prev = jnp.where(row == 0, prev_row, pltpu.roll(z, 1, 0))
    zs = z + (prev - z) * mu_ref[...]
    r = zs[:, 0:W]
    kr = zs[:, W:2 * W]
    vr = zs[:, 2 * W:3 * W]
    xwa = zs[:, 3 * W:3 * W + DECAY_LORA + AAA_LORA]
    xg = zs[:, 3 * W + DECAY_LORA + AAA_LORA:]

    dw = w0_ref[...] + _dot(jnp.tanh(xwa), w2_ref[...])
    softplus = jnp.maximum(-dw, 0.0) + jnp.log1p(jnp.exp(-jnp.abs(dw)))
    lw_ref[0] = -jnp.exp(-softplus - 0.5)
    asig = jax.nn.sigmoid(a0_ref[...] + _dot(xwa, a2_ref[...]))
    g = _dot(jax.nn.sigmoid(xg), g2_ref[...])

    seg = seg_ref[...]
    kk = kr * kk_ref[...]
    norm = jnp.sqrt(_dot_hilo_rhs_exact(kk * kk, seg))
    kkn = kk / jnp.maximum(norm, 1e-12)
    kmod = kr * (1.0 + (asig - 1.0) * ka_ref[...])
    bonus = _dot_hilo_rhs_exact(r * kmod * rk_ref[...], seg) * vr

    r_ref[0] = r
    k_ref[0] = kmod
    v_ref[0] = vr
    a_ref[0] = -kkn
    b_ref[0] = kkn * asig
    g_ref[0] = g
    bg_ref[0] = bonus * g


def _dot_hilo_rhs_exact(x, seg):
    hi = x.astype(BF16)
    lo = (x - hi.astype(F32)).astype(BF16)
    return (jnp.dot(hi, seg, preferred_element_type=F32) + jnp.dot(lo, seg, preferred_element_type=F32))


def _rwkv_prep(zr, mu, w0, w2, a0, a2, g2, kkp, ka, rk, tm):
    B, S, C = zr.shape
    W = RWKV_WIDTH
    w2p = jnp.concatenate([w2, jnp.zeros_like(w2)], axis=0).astype(BF16)
    a2p = jnp.concatenate([jnp.zeros_like(a2), a2], axis=0).astype(BF16)
    head = jnp.arange(W) // RWKV_HEAD_DIM
    seg = (head[:, None] == head[None, :]).astype(BF16)
    row = lambda v: v.reshape(1, -1).astype(F32)
    nblk8 = tm // 8
    out_spec = pl.BlockSpec((1, tm, W), lambda b, i: (b, i, 0))
    return pl.pallas_call(
        _rwkv_prep_kernel,
        grid=(B, S // tm),
        in_specs=[pl.BlockSpec((1, tm, C), lambda b, i: (b, i, 0)),
                  pl.BlockSpec((1, 8, C), lambda b, i: (b, jnp.maximum(i * nblk8 - 1, 0), 0)),
                  _const_spec((1, C)), _const_spec((1, W)), _const_spec((LANES, W)),
                  _const_spec((1, W)), _const_spec((LANES, W)), _const_spec((GATE_LORA, W)),
                  _const_spec((1, W)), _const_spec((1, W)), _const_spec((1, W)),
                  _const_spec((W, W))],
        out_specs=[out_spec] * 8,
        out_shape=[jax.ShapeDtypeStruct((B, S, W), F32)] * 8,
        compiler_params=_cparams("parallel", "parallel"),
        name="rwkv_prep",
    )(zr, zr, row(mu), row(w0), w2p, row(a0), a2p, g2.astype(BF16), row(kkp), row(ka), row(rk), seg)


def _scan_local(chunks, consts):
    C = SCAN_CHUNK
    n = len(chunks)
    tril, strict2, incl2, lane_lo, lane_hi, bd = consts
    zc = jnp.zeros((C, LANES), F32)
    cat0 = lambda *xs: jnp.concatenate(xs, axis=0)
    cat1 = lambda *xs: jnp.concatenate(xs, axis=1)

    L_all = _dot_hilo(tril, cat1(*[ch[1] for ch in chunks]))
    pre = []
    for i, (r, lw, k, v, a, b) in enumerate(chunks):
        L = L_all[:, i * LANES:(i + 1) * LANES]
        winv = jnp.exp(-L)
        l_end = L[C - 1:C, :]
        wend = jnp.exp(l_end - L)
        rt = r * jnp.exp(L)
        at = a * jnp.exp(L - lw)
        bk = cat0(b * winv, k * winv)
        bk_end = cat0(b * wend, k * wend)
        pre.append((rt, at, bk, bk_end, jnp.exp(l_end)))

    outs = []
    for rt, at, bk, _, _ in pre:
        ar = cat0(at, rt)
        zero = jnp.zeros_like(ar)
        outs.append(_dot_nt(cat0(jnp.where(lane_lo, ar, zero), jnp.where(lane_hi, ar, zero)), bk))
    tops = [(jnp.where(strict2, o[0:C], 0.0), jnp.where(strict2, o[2 * C:3 * C], 0.0)) for o in outs]
    bots = [cat0(jnp.where(incl2, o[C:2 * C], 0.0), jnp.where(incl2, o[3 * C:4 * C], 0.0)) for o in outs]

    aak_v = [_dot(cat0(jnp.where(lane_hi, t0, zc), jnp.where(lane_hi, t1, zc)), cat0(ch[3], ch[3]))
             for (t0, t1), ch in zip(tops, chunks)]
    A = [cat0(jnp.where(lane_lo, t0, zc), jnp.where(lane_hi, pltpu.roll(t1, C, 1), zc)) for t0, t1 in tops]
    Z = [cat1(cat0(p[1], p[1]), av) for p, av in zip(pre, aak_v)]
    n_steps = int(math.log2(C))
    for s in range(n_steps):
        Z = [z + _dot(a_, z) for a_, z in zip(A, Z)]
        if s + 1 < n_steps:
            A = [_dot(a_, a_) for a_ in A]
    ahat = [jnp.where(lane_lo, z[0:C, 0:LANES], z[C:2 * C, 0:LANES]) for z in Z]
    vhat = [jnp.where(lane_lo, z[0:C, LANES:], z[C:2 * C, LANES:]) for z in Z]

    yr2 = [_dot(bot, cat0(cat1(vh, ah), cat1(ch[3], zc)))
           for bot, vh, ah, ch in zip(bots, vhat, ahat, chunks)]
    qn = [_dot_tn(cat0(cat1(ah, vh), cat1(zc, ch[3])), p[3])
          for ah, vh, ch, p in zip(ahat, vhat, chunks, pre)]
    res = []
    for i in range(n):
        yhat = jnp.where(lane_lo, yr2[i][0:C, 0:LANES], yr2[i][C:2 * C, 0:LANES])
        rhat = pre[i][0] + jnp.where(lane_lo, yr2[i][0:C, LANES:], yr2[i][C:2 * C, LANES:])
        Q = jnp.where(bd, qn[i][0:LANES], 0.0)
        Nt = jnp.where(bd, qn[i][LANES:], 0.0)
        res.append((rhat, yhat, Q, Nt, pre[i][4]))
    return res


def _rwkv_scan_kernel(r_ref, lw_ref, k_ref, v_ref, a_ref, b_ref, g_ref, bg_ref, lnw_ref, lnb_ref,
                      o_ref, s_ref):
    C = SCAN_CHUNK
    n_chunks = r_ref.shape[1] // C

    @pl.when(pl.program_id(2) == 0)
    def _():
        s_ref[...] = jnp.zeros(s_ref.shape, F32)

    ri = lax.broadcasted_iota(jnp.int32, (C, 2 * C), 0)
    ci = lax.broadcasted_iota(jnp.int32, (C, 2 * C), 1) % C
    strict2 = ci < ri
    incl2 = ci <= ri
    lane = lax.broadcasted_iota(jnp.int32, (1, LANES), 1)
    lane_lo = lane < RWKV_HEAD_DIM
    lane_hi = lane >= RWKV_HEAD_DIM
    r128 = lax.broadcasted_iota(jnp.int32, (LANES, LANES), 0)
    c128 = lax.broadcasted_iota(jnp.int32, (LANES, LANES), 1)
    bd = (r128 < RWKV_HEAD_DIM) == (c128 < RWKV_HEAD_DIM)
    tril = (lax.broadcasted_iota(jnp.int32, (C, C), 1) <= lax.broadcasted_iota(jnp.int32, (C, C), 0)).astype(BF16)
    seg_mean = jnp.where(bd, 1.0, 0.0).astype(BF16)
    consts = (tril, strict2, incl2, lane_lo, lane_hi, bd)

    chunks = []
    for c in range(n_chunks):
        sl = slice(c * C, (c + 1) * C)
        chunks.append((r_ref[0, sl, :], lw_ref[0, sl, :], k_ref[0, sl, :], v_ref[0, sl, :],
                       a_ref[0, sl, :], b_ref[0, sl, :]))
    local = _scan_local(chunks, consts)

    S = s_ref[...]
    inv_n = 1.0 / RWKV_HEAD_DIM
    for c in range(n_chunks):
        sl = slice(c * C, (c + 1) * C)
        rhat, yhat, Q, Nt, wc = local[c]
        y = _dot_nt(rhat, S) + yhat
        S = S * wc + _dot(S, Q) + Nt
        mean = _dot_hilo_rhs_exact(y, seg_mean) * inv_n
        d = y - mean
        var = _dot_hilo_rhs_exact(d * d, seg_mean) * inv_n
        yn = d * lax.rsqrt(var + GN_EPS) * lnw_ref[...] + lnb_ref[...]
        o_ref[0, sl, :] = (yn * g_ref[0, sl, :] + bg_ref[0, sl, :]).astype(o_ref.dtype)
    s_ref[...] = S


def _rwkv_scan(r, lw, k, v, a, b, g, bg, lnw, lnb, tm):
    B, S, W = r.shape
    n_pairs = W // LANES
    spec = pl.BlockSpec((1, tm, LANES), lambda bb, p, i: (bb, i, p))
    vec = pl.BlockSpec((1, LANES), lambda bb, p, i: (0, p))
    return pl.pallas_call(
        _rwkv_scan_kernel,
        grid=(B, n_pairs, S // tm),
        in_specs=[spec] * 8 + [vec, vec],
        out_specs=spec,
        out_shape=jax.ShapeDtypeStruct((B, S, W), BF16),
        scratch_shapes=[pltpu.VMEM((LANES, LANES), F32)],
        compiler_params=_cparams("parallel", "parallel", "arbitrary"),
        name="rwkv_scan",
    )(r, lw, k, v, a, b, g, bg, lnw.reshape(1, W).astype(F32), lnb.reshape(1, W).astype(F32))


def _post_kernel(x_ref, oa_ref, yr_ref, gates_ref, p_ref, woa_ref, wor_ref, wout_ref, nffn_ref,
                 wg_ref, wu_ref, wd_ref, nple_ref, wple_ref, wpg_ref, nfin_ref, o_ref, *, final, ff_chunk):
    D = x_ref.shape[1]
    ya = jnp.dot(oa_ref[...], woa_ref[...], preferred_element_type=F32)
    yr = jnp.dot(yr_ref[...], wor_ref[...], preferred_element_type=F32)
    m = gates_ref[:, 0:D].astype(F32) * ya + gates_ref[:, D:2 * D].astype(F32) * yr
    x = x_ref[...] + _dot(m, wout_ref[...])

    h2 = _rms(x, nffn_ref[...], NORM_EPS).astype(BF16)
    d_ff = wg_ref.shape[1]
    acc = jnp.zeros_like(x)
    for c0 in range(0, d_ff, ff_chunk):
        gt = jnp.dot(h2, wg_ref[:, c0:c0 + ff_chunk], preferred_element_type=F32)
        up = jnp.dot(h2, wu_ref[:, c0:c0 + ff_chunk], preferred_element_type=F32)
        act = gt * jax.nn.sigmoid(gt) * up
        acc = acc + jnp.dot(act.astype(BF16), wd_ref[c0:c0 + ff_chunk, :], preferred_element_type=F32)
    x = x + acc

    e = _dot(p_ref[...], wple_ref[...])
    gp = jax.nn.sigmoid(_dot(_rms(x, nple_ref[...], NORM_EPS), wpg_ref[...]))
    x = x + gp * e
    if final:
        x = _rms(x, nfin_ref[...], NORM_EPS)
    o_ref[...] = x


def _post(x2d, oa, yr, gates, p2d, woa, wor, wout, nffn, wg, wu, wd, nple, wple, wpg, nfin, final, tm):
    T, D = x2d.shape
    d_ff = wg.shape[1]
    ff_chunk = 256 if d_ff % 256 == 0 else LANES
    tok = lambda w: pl.BlockSpec((tm, w), lambda i: (i, 0))
    row = lambda v: v.reshape(1, -1).astype(F32)
    bf = lambda w: w.astype(BF16)
    return pl.pallas_call(
        functools.partial(_post_kernel, final=final, ff_chunk=ff_chunk),
        grid=(T // tm,),
        in_specs=[tok(D), tok(oa.shape[1]), tok(yr.shape[1]), tok(gates.shape[1]), tok(p2d.shape[1]),
                  _const_spec(woa.shape), _const_spec(wor.shape), _const_spec(wout.shape),
                  _const_spec((1, D)), _const_spec(wg.shape), _const_spec(wu.shape), _const_spec(wd.shape),
                  _const_spec((1, D)), _const_spec(wple.shape), _const_spec(wpg.shape), _const_spec((1, D))],
        out_specs=tok(D),
        out_shape=jax.ShapeDtypeStruct((T, D), F32),
        compiler_params=_cparams("parallel"),
        name="post",
    )(x2d, oa, yr, gates, p2d, bf(woa), bf(wor), bf(wout), row(nffn), bf(wg), bf(wu), bf(wd),
      row(nple), bf(wple), bf(wpg), row(nfin))


def _tile(n, pref):
    t = min(n, pref)
    assert n % t == 0, (n, t)
    return t


def kernel(x, p, rel_bias, norm_mix, w_in, lam_q1, lam_k1, lam_q2, lam_k2, attn_subln, rwkv_mu, rwkv_w0, rwkv_w2, rwkv_a0, rwkv_a2, rwkv_g2, rwkv_kk, rwkv_ka, rwkv_rk, rwkv_lnx_w, rwkv_lnx_b, w_out_attn, w_out_rwkv, w_out, norm_ffn, w_ffn_gate, w_ffn_up, w_ffn_down, norm_ple, w_ple, w_ple_gate, norm_final):
    B, S, D = x.shape
    depth = w_in.shape[0]
    T = B * S
    t_attn = _tile(S, ATTN_TILE)
    assert t_attn >= 113
    tm_tok = _tile(T, 512)
    tm_prep = _tile(S, 256)
    tm_scan = _tile(S, 8 * SCAN_CHUNK)
    assert tm_scan % SCAN_CHUNK == 0

    bias_tiles = _bias_tiles(rel_bias, t_attn)
    x2d = x.reshape(T, D)
    for i in range(depth):
        qkv, zr, gates = _inproj(x2d, norm_mix[i], w_in[i].astype(BF16), tm_tok)

        lam_init = 0.8 - 0.6 * math.exp(-0.3 * i)
        lam = (jnp.exp(jnp.sum(lam_q1[i] * lam_k1[i])) - jnp.exp(jnp.sum(lam_q2[i] * lam_k2[i])) + lam_init)
        scal = jnp.stack([lam.astype(F32), jnp.asarray(1.0 - lam_init, F32)])
        oa = _attention(qkv.reshape(B, S, -1), bias_tiles, scal, attn_subln[i], t_attn, ATTN_FAR_TILES)

        r, lw, k, v, a, b, g, bg = _rwkv_prep(zr.reshape(B, S, -1), rwkv_mu[i], rwkv_w0[i], rwkv_w2[i],
                                              rwkv_a0[i], rwkv_a2[i], rwkv_g2[i], rwkv_kk[i], rwkv_ka[i],
                                              rwkv_rk[i].reshape(-1), tm_prep)
        yr = _rwkv_scan(r, lw, k, v, a, b, g, bg, rwkv_lnx_w[i], rwkv_lnx_b[i], tm_scan)

        x2d = _post(x2d, oa.reshape(T, -1), yr.reshape(T, -1), gates, p[i].reshape(T, -1),
                    w_out_attn[i], w_out_rwkv[i], w_out[i], norm_ffn[i], w_ffn_gate[i], w_ffn_up[i],
                    w_ffn_down[i], norm_ple[i], w_ple[i], w_ple_gate[i], norm_final,
                    final=(i == depth - 1), tm=tm_tok)
    return x2d.reshape(B, S, D)
```

```python
import functools
import math

import jax
import jax.numpy as jnp
from jax import lax
from jax.experimental import pallas as pl
from jax.experimental.pallas import tpu as pltpu

F32 = jnp.float32
BF16 = jnp.bfloat16

N_ATTN_HEADS = 4
ATTN_HALF_DIM = 64
ATTN_V_DIM = 128
ATTN_WIDTH = 512
N_RWKV_HEADS = 8
RWKV_HEAD_DIM = 64
RWKV_WIDTH = 512
DECAY_LORA = 64
AAA_LORA = 64
GATE_LORA = 128
RWKV_COLS = 3 * RWKV_WIDTH + DECAY_LORA + AAA_LORA + GATE_LORA
NUM_BUCKETS = 32
MAX_DISTANCE = 128
NORM_EPS = 1e-6
SUBLN_EPS = 1e-5
GN_EPS = 64e-5

LANES = 128
VMEM_LIMIT_BYTES = 56 * 1024 * 1024
ATTN_TILE = 512
ATTN_FAR_TILES = 2
ATTN_ROW_SPLIT = 2
LOG2_E = math.log2(math.e)
SCAN_CHUNK = 64
MASK_VALUE = -1e30


def _cparams(*sem):
    return pltpu.CompilerParams(dimension_semantics=sem, vmem_limit_bytes=VMEM_LIMIT_BYTES)


def _const_spec(shape):
    nd = len(shape)
    return pl.BlockSpec(shape, lambda *_: (0,) * nd, pipeline_mode=pl.Buffered(1))


def _dot(a, b):
    return jnp.dot(a.astype(BF16), b.astype(BF16), preferred_element_type=F32)


def _dot_nt(a, b):
    return lax.dot_general(a.astype(BF16), b.astype(BF16), (((1,), (1,)), ((), ())),
                           preferred_element_type=F32)


def _dot_tn(a, b):
    return lax.dot_general(a.astype(BF16), b.astype(BF16), (((0,), (0,)), ((), ())),
                           preferred_element_type=F32)


def _dot_hilo(a_exact_bf16, x):
    hi = x.astype(BF16)
    lo = (x - hi.astype(F32)).astype(BF16)
    return (jnp.dot(a_exact_bf16, hi, preferred_element_type=F32)
            + jnp.dot(a_exact_bf16, lo, preferred_element_type=F32))


def _rms(x, g, eps):
    return x * lax.rsqrt(jnp.mean(x * x, axis=-1, keepdims=True) + eps) * g


def _inproj_kernel(x_ref, g_ref, w_ref, qkv_ref, zr_ref, gates_ref):
    h = _rms(x_ref[...], g_ref[...], NORM_EPS).astype(BF16)
    scale = ATTN_HALF_DIM ** -0.5 * LOG2_E
    nq = 3 * ATTN_WIDTH
    for c0 in range(0, nq, 512):
        z = jnp.dot(h, w_ref[:, c0:c0 + 512], preferred_element_type=F32)
        if c0 < ATTN_WIDTH:
            z = z * scale
        qkv_ref[:, c0:c0 + 512] = z.astype(BF16)
    c0 = nq
    while c0 < nq + RWKV_COLS:
        w = min(512, nq + RWKV_COLS - c0)
        zr_ref[:, c0 - nq:c0 - nq + w] = jnp.dot(h, w_ref[:, c0:c0 + w], preferred_element_type=F32)
        c0 += w
    base = nq + RWKV_COLS
    for c0 in range(0, gates_ref.shape[1], 512):
        z = jnp.dot(h, w_ref[:, base + c0:base + c0 + 512], preferred_element_type=F32)
        gates_ref[:, c0:c0 + 512] = jax.nn.sigmoid(z).astype(BF16)


def _inproj(x2d, g, w_bf16, tm):
    T, D = x2d.shape
    ncols = w_bf16.shape[1]
    ngate = ncols - 3 * ATTN_WIDTH - RWKV_COLS
    return pl.pallas_call(
        _inproj_kernel,
        grid=(T // tm,),
        in_specs=[pl.BlockSpec((tm, D), lambda i: (i, 0)),
                  _const_spec((1, D)),
                  _const_spec((D, ncols))],
        out_specs=[pl.BlockSpec((tm, 3 * ATTN_WIDTH), lambda i: (i, 0)),
                   pl.BlockSpec((tm, RWKV_COLS), lambda i: (i, 0)),
                   pl.BlockSpec((tm, ngate), lambda i: (i, 0))],
        out_shape=[jax.ShapeDtypeStruct((T, 3 * ATTN_WIDTH), BF16),
                   jax.ShapeDtypeStruct((T, RWKV_COLS), F32),
                   jax.ShapeDtypeStruct((T, ngate), BF16)],
        compiler_params=_cparams("parallel"),
        name="inproj",
    )(x2d, g.reshape(1, D), w_bf16)


def _bias_tiles_kernel(rb_ref, out_ref):
    hc = pl.program_id(0)
    t = out_ref.shape[-1]
    rows = lax.broadcasted_iota(jnp.int32, (t, t), 0)
    cols = lax.broadcasted_iota(jnp.int32, (t, t), 1)
    max_exact = NUM_BUCKETS // 2
    n_hc = 2 * N_ATTN_HEADS
    far = rb_ref[(NUM_BUCKETS - 1) * n_hc + hc]
    for kind in range(2):
        dist = kind * t + rows - cols
        n = jnp.maximum(dist, 0)
        nf = jnp.maximum(n, max_exact).astype(F32)
        large = max_exact + (jnp.log(nf / max_exact) / math.log(MAX_DISTANCE / max_exact)
                             * (NUM_BUCKETS - max_exact)).astype(jnp.int32)
        large = jnp.minimum(large, NUM_BUCKETS - 1)
        bucket = jnp.where(n < max_exact, n, large)
        tile = jnp.zeros((t, t), F32)
        for b in range(NUM_BUCKETS):
            tile = jnp.where(bucket == b, rb_ref[b * n_hc + hc], tile)
        tile = (tile - far) * LOG2_E
        if kind == 0:
            tile = jnp.where(dist >= 0, tile, MASK_VALUE)
        out_ref[0, kind] = tile


def _bias_tiles(rel_bias, t):
    n_hc = 2 * N_ATTN_HEADS
    return pl.pallas_call(
        _bias_tiles_kernel,
        grid=(n_hc,),
        in_specs=[pl.BlockSpec(memory_space=pltpu.SMEM)],
        out_specs=pl.BlockSpec((1, 2, t, t), lambda i: (i, 0, 0, 0)),
        out_shape=jax.ShapeDtypeStruct((n_hc, 2, t, t), F32),
        compiler_params=_cparams("parallel"),
        name="bias_tiles",
    )(rel_bias.reshape(-1).astype(F32))


def _attn_kernel(scal_ref, q_ref, k_ref, v_ref, bias_ref, g_ref, o_ref, m_ref, acc_ref, v1_ref, *,
                 far_tiles, row_split):
    t = q_ref.shape[1]
    tr = t // row_split
    qi = pl.program_id(2)

    @pl.when(qi == 0)
    def _():
        v1_ref[:, 0:ATTN_V_DIM] = v_ref[0]
        v1_ref[:, ATTN_V_DIM:] = jnp.ones((v1_ref.shape[0], LANES), BF16)

    lane = lax.broadcasted_iota(jnp.int32, (1, LANES), 1)
    q = q_ref[0]
    zero = jnp.zeros_like(q)
    qm = (jnp.where(lane < ATTN_HALF_DIM, q, zero), jnp.where(lane >= ATTN_HALF_DIM, q, zero))

    m_ref[...] = jnp.full(m_ref.shape, MASK_VALUE, F32)
    acc_ref[...] = jnp.zeros(acc_ref.shape, F32)

    streams = [(c, r) for c in range(2) for r in range(row_split)]

    def step(start, width, kind):
        kb = k_ref[0, pl.ds(start, width), :]
        vb = v1_ref[pl.ds(start, width), :]
        scores = []
        for c, r in streams:
            s = lax.dot_general(qm[c][r * tr:(r + 1) * tr], kb, (((1,), (1,)), ((), ())),
                                preferred_element_type=F32)
            if kind is not None:
                s = s + bias_ref[0, c, kind, r * tr:(r + 1) * tr, :]
            scores.append(s)
        for (c, r), s in zip(streams, scores):
            rows = slice(r * tr, (r + 1) * tr)
            m_prev = m_ref[c, rows, :]
            m_new = jnp.maximum(m_prev, jnp.max(s, axis=-1, keepdims=True))
            alpha = jnp.exp2(m_prev - m_new)
            p = jnp.exp2(s - jnp.concatenate([m_new] * (width // LANES), axis=1))
            acc_ref[c, rows, :] = (jnp.concatenate([alpha, alpha], axis=1) * acc_ref[c, rows, :]
                                   + jnp.dot(p.astype(BF16), vb, preferred_element_type=F32))
            m_ref[c, rows, :] = m_new

    n_far = jnp.maximum(qi - 1, 0)
    far_w = far_tiles * t

    def far_body(j, carry):
        step(pl.multiple_of(j * far_w, far_w), far_w, None)
        return carry

    n_wide = n_far // far_tiles
    lax.fori_loop(0, n_wide, far_body, 0)
    for extra in range(far_tiles - 1):
        @pl.when(n_far - n_wide * far_tiles > extra)
        def _():
            step(pl.multiple_of((n_wide * far_tiles + extra) * t, t), t, None)

    @pl.when(qi >= 1)
    def _():
        step(pl.multiple_of((qi - 1) * t, t), t, 1)

    step(pl.multiple_of(qi * t, t), t, 0)

    lam = scal_ref[0]
    out_scale = scal_ref[1]
    dv = ATTN_V_DIM
    o = acc_ref[0, :, 0:dv] / acc_ref[0, :, dv:] - lam * (acc_ref[1, :, 0:dv] / acc_ref[1, :, dv:])
    o_ref[0] = (_rms(o, g_ref[...], SUBLN_EPS) * out_scale).astype(o_ref.dtype)


def _attention(qkv, bias_tiles, scal, subln_g, t, far_tiles):
    B, S, _ = qkv.shape
    H = N_ATTN_HEADS
    return pl.pallas_call(
        functools.partial(_attn_kernel, far_tiles=far_tiles, row_split=ATTN_ROW_SPLIT),
        grid=(B, H, S // t),
        in_specs=[pl.BlockSpec(memory_space=pltpu.SMEM),
                  pl.BlockSpec((1, t, LANES), lambda b, h, i: (b, i, h)),
                  pl.BlockSpec((1, S, LANES), lambda b, h, i: (b, 0, H + h)),
                  pl.BlockSpec((1, S, LANES), lambda b, h, i: (b, 0, 2 * H + h)),
                  pl.BlockSpec((1, 2, 2, t, t), lambda b, h, i: (h, 0, 0, 0, 0)),
                  _const_spec((1, ATTN_V_DIM))],
        out_specs=pl.BlockSpec((1, t, LANES), lambda b, h, i: (b, i, h)),
        out_shape=jax.ShapeDtypeStruct((B, S, ATTN_WIDTH), BF16),
        scratch_shapes=[pltpu.VMEM((2, t, LANES), F32),
                        pltpu.VMEM((2, t, ATTN_V_DIM + LANES), F32),
                        pltpu.VMEM((S, ATTN_V_DIM + LANES), BF16)],
        compiler_params=_cparams("parallel", "parallel", "arbitrary"),
        name="diff_attention",
    )(scal, qkv, qkv, qkv, bias_tiles.reshape(H, 2, 2, t, t), subln_g.reshape(1, ATTN_V_DIM))


def _rwkv_prep_kernel(z_ref, zp_ref, mu_ref, w0_ref, w2_ref, a0_ref, a2_ref, g2_ref, kk_ref, ka_ref,
                      rk_ref, seg_ref,
                      r_ref, lw_ref, k_ref, v_ref, a_ref, b_ref, g_ref, bg_ref):
    i = pl.program_id(1)
    z = z_ref[0]
    tm = z.shape[0]
    W = RWKV_WIDTH
    prev_row = zp_ref[0, 7:8, :] * (i > 0).astype(F32)
    row = lax.broadcasted_iota(jnp.int32, (tm, 1), 0)
    prev = jnp.where(row == 0, prev_row, pltpu.roll(z, 1, 0))
    zs = z + (prev - z) * mu_ref[...]
    r = zs[:, 0:W]
    kr = zs[:, W:2 * W]
    vr = zs[:, 2 * W:3 * W]
    xwa = zs[:, 3 * W:3 * W + DECAY_LORA + AAA_LORA]
    xg = zs[:, 3 * W + DECAY_LORA + AAA_LORA:]

    dw = w0_ref[...] + _dot(jnp.tanh(xwa), w2_ref[...])
    softplus = jnp.maximum(-dw, 0.0) + jnp.log1p(jnp.exp(-jnp.abs(dw)))
    lw_ref[0] = -jnp.exp(-softplus - 0.5)
    asig = jax.nn.sigmoid(a0_ref[...] + _dot(xwa, a2_ref[...]))
    g = _dot(jax.nn.sigmoid(xg), g2_ref[...])

    seg = seg_ref[...]
    kk = kr * kk_ref[...]
    norm = jnp.sqrt(_dot_hilo_rhs_exact(kk * kk, seg))
    kkn = kk / jnp.maximum(norm, 1e-12)
    kmod = kr * (1.0 + (asig - 1.0) * ka_ref[...])
    bonus = _dot_hilo_rhs_exact(r * kmod * rk_ref[...], seg) * vr

    r_ref[0] = r
    k_ref[0] = kmod
    v_ref[0] = vr
    a_ref[0] = -kkn
    b_ref[0] = kkn * asig
    g_ref[0] = g
    bg_ref[0] = bonus * g


def _dot_hilo_rhs_exact(x, seg):
    hi = x.astype(BF16)
    lo = (x - hi.astype(F32)).astype(BF16)
    return (jnp.dot(hi, seg, preferred_element_type=F32) + jnp.dot(lo, seg, preferred_element_type=F32))


def _rwkv_prep(zr, mu, w0, w2, a0, a2, g2, kkp, ka, rk, tm):
    B, S, C = zr.shape
    W = RWKV_WIDTH
    w2p = jnp.concatenate([w2, jnp.zeros_like(w2)], axis=0).astype(BF16)
    a2p = jnp.concatenate([jnp.zeros_like(a2), a2], axis=0).astype(BF16)
    head = jnp.arange(W) // RWKV_HEAD_DIM
    seg = (head[:, None] == head[None, :]).astype(BF16)
    row = lambda v: v.reshape(1, -1).astype(F32)
    nblk8 = tm // 8
    out_spec = pl.BlockSpec((1, tm, W), lambda b, i: (b, i, 0))
    return pl.pallas_call(
        _rwkv_prep_kernel,
        grid=(B, S // tm),
        in_specs=[pl.BlockSpec((1, tm, C), lambda b, i: (b, i, 0)),
                  pl.BlockSpec((1, 8, C), lambda b, i: (b, jnp.maximum(i * nblk8 - 1, 0), 0)),
                  _const_spec((1, C)), _const_spec((1, W)), _const_spec((LANES, W)),
                  _const_spec((1, W)), _const_spec((LANES, W)), _const_spec((GATE_LORA, W)),
                  _const_spec((1, W)), _const_spec((1, W)), _const_spec((1, W)),
                  _const_spec((W, W))],
        out_specs=[out_spec] * 8,
        out_shape=[jax.ShapeDtypeStruct((B, S, W), F32)] * 8,
        compiler_params=_cparams("parallel", "parallel"),
        name="rwkv_prep",
    )(zr, zr, row(mu), row(w0), w2p, row(a0), a2p, g2.astype(BF16), row(kkp), row(ka), row(rk), seg)


def _scan_local(chunks, consts, tick):
    C = SCAN_CHUNK
    n = len(chunks)
    tril, strict2, incl2, lane_lo, lane_hi, bd = consts
    zc = jnp.zeros((C, LANES), F32)
    cat0 = lambda *xs: jnp.concatenate(xs, axis=0)
    cat1 = lambda *xs: jnp.concatenate(xs, axis=1)

    L_all = _dot_hilo(tril, cat1(*[ch[1] for ch in chunks]))
    pre = []
    for i, (r, lw, k, v, a, b) in enumerate(chunks):
        L = L_all[:, i * LANES:(i + 1) * LANES]
        winv = jnp.exp(-L)
        l_end = L[C - 1:C, :]
        wend = jnp.exp(l_end - L)
        rt = r * jnp.exp(L)
        at = a * jnp.exp(L - lw)
        bk = cat0(b * winv, k * winv)
        bk_end = cat0(b * wend, k * wend)
        pre.append((rt, at, bk, bk_end, jnp.exp(l_end)))
    tick()

    outs = []
    for rt, at, bk, _, _ in pre:
        ar = cat0(at, rt)
        zero = jnp.zeros_like(ar)
        outs.append(_dot_nt(cat0(jnp.where(lane_lo, ar, zero), jnp.where(lane_hi, ar, zero)), bk))
    tops = [(jnp.where(strict2, o[0:C], 0.0), jnp.where(strict2, o[2 * C:3 * C], 0.0)) for o in outs]
    bots = [cat0(jnp.where(incl2, o[C:2 * C], 0.0), jnp.where(incl2, o[3 * C:4 * C], 0.0)) for o in outs]
    tick()

    aak_v = [_dot(cat0(jnp.where(lane_hi, t0, zc), jnp.where(lane_hi, t1, zc)), cat0(ch[3], ch[3]))
             for (t0, t1), ch in zip(tops, chunks)]
    tick()
    A = [cat0(jnp.where(lane_lo, t0, zc), jnp.where(lane_hi, pltpu.roll(t1, C, 1), zc)) for t0, t1 in tops]
    Z = [cat1(cat0(p[1], p[1]), av) for p, av in zip(pre, aak_v)]
    n_steps = int(math.log2(C))
    for s in range(n_steps):
        Z = [z + _dot(a_, z) for a_, z in zip(A, Z)]
        tick()
        if s + 1 < n_steps:
            A = [_dot(a_, a_) for a_ in A]
            tick()
    ahat = [jnp.where(lane_lo, z[0:C, 0:LANES], z[C:2 * C, 0:LANES]) for z in Z]
    vhat = [jnp.where(lane_lo, z[0:C, LANES:], z[C:2 * C, LANES:]) for z in Z]

    yr2 = [_dot(bot, cat0(cat1(vh, ah), cat1(ch[3], zc)))
           for bot, vh, ah, ch in zip(bots, vhat, ahat, chunks)]
    tick()
    qn = [_dot_tn(cat0(cat1(ah, vh), cat1(zc, ch[3])), p[3])
          for ah, vh, ch, p in zip(ahat, vhat, chunks, pre)]
    tick()
    res = []
    for i in range(n):
        yhat = jnp.where(lane_lo, yr2[i][0:C, 0:LANES], yr2[i][C:2 * C, 0:LANES])
        rhat = pre[i][0] + jnp.where(lane_lo, yr2[i][0:C, LANES:], yr2[i][C:2 * C, LANES:])
        Q = jnp.where(bd, qn[i][0:LANES], 0.0)
        Nt = jnp.where(bd, qn[i][LANES:], 0.0)
        res.append((rhat, yhat, Q, Nt, pre[i][4]))
    return res


SCAN_LOCAL_STAGES = 5 + 2 * int(math.log2(SCAN_CHUNK)) - 1


def _rwkv_scan_kernel(r_ref, lw_ref, k_ref, v_ref, a_ref, b_ref, g_ref, bg_ref, lnw_ref, lnb_ref,
                      o_ref, s_ref, rh_ref, yh_ref, q_ref, n_ref, wc_ref):
    C = SCAN_CHUNK
    n_chunks = r_ref.shape[1] // C
    i = pl.program_id(2)
    n_blocks = pl.num_programs(2) - 1

    @pl.when(i == 0)
    def _():
        s_ref[...] = jnp.zeros(s_ref.shape, F32)
        rh_ref[...] = jnp.zeros(rh_ref.shape, rh_ref.dtype)
        yh_ref[...] = jnp.zeros(yh_ref.shape, F32)
        q_ref[...] = jnp.zeros(q_ref.shape, q_ref.dtype)
        n_ref[...] = jnp.zeros(n_ref.shape, F32)
        wc_ref[...] = jnp.zeros(wc_ref.shape, F32)

    ri = lax.broadcasted_iota(jnp.int32, (C, 2 * C), 0)
    ci = lax.broadcasted_iota(jnp.int32, (C, 2 * C), 1) % C
    strict2 = ci < ri
    incl2 = ci <= ri
    lane = lax.broadcasted_iota(jnp.int32, (1, LANES), 1)
    lane_lo = lane < RWKV_HEAD_DIM
    lane_hi = lane >= RWKV_HEAD_DIM
    r128 = lax.broadcasted_iota(jnp.int32, (LANES, LANES), 0)
    c128 = lax.broadcasted_iota(jnp.int32, (LANES, LANES), 1)
    bd = (r128 < RWKV_HEAD_DIM) == (c128 < RWKV_HEAD_DIM)
    tril = (lax.broadcasted_iota(jnp.int32, (C, C), 1) <= lax.broadcasted_iota(jnp.int32, (C, C), 0)).astype(BF16)
    consts = (tril, strict2, incl2, lane_lo, lane_hi, bd)
    inv_n = 1.0 / RWKV_HEAD_DIM

    def head_sums(x):
        lo = jnp.sum(jnp.where(lane_lo, x, 0.0), axis=-1, keepdims=True)
        hi = jnp.sum(jnp.where(lane_hi, x, 0.0), axis=-1, keepdims=True)
        return jnp.where(lane_lo, lo, hi)

    def recurrence_step(c, S):
        sl = slice(c * C, (c + 1) * C)
        y = _dot_nt(rh_ref[c], S) + yh_ref[c]
        S = S * wc_ref[c] + _dot(S, q_ref[c]) + n_ref[c]
        mean = head_sums(y) * inv_n
        d = y - mean
        var = head_sums(d * d) * inv_n
        yn = d * lax.rsqrt(var + GN_EPS) * lnw_ref[...] + lnb_ref[...]
        o_ref[0, sl, :] = (yn * g_ref[0, sl, :] + bg_ref[0, sl, :]).astype(o_ref.dtype)
        return S

    @pl.when(i < n_blocks)
    def _():
        state = [s_ref[...]]
        done = [0]
        ticks = [0]

        def tick():
            ticks[0] += 1
            target = min(n_chunks, (ticks[0] * n_chunks + SCAN_LOCAL_STAGES - 1) // SCAN_LOCAL_STAGES)
            while done[0] < target:
                state[0] = recurrence_step(done[0], state[0])
                done[0] += 1

        chunks = []
        for c in range(n_chunks):
            sl = slice(c * C, (c + 1) * C)
            chunks.append((r_ref[0, sl, :], lw_ref[0, sl, :], k_ref[0, sl, :], v_ref[0, sl, :],
                           a_ref[0, sl, :], b_ref[0, sl, :]))
        local = _scan_local(chunks, consts, tick)
        assert done[0] == n_chunks
        s_ref[...] = state[0]
        for c in range(n_chunks):
            rhat, yhat, Q, Nt, wc = local[c]
            rh_ref[c] = rhat.astype(rh_ref.dtype)
            yh_ref[c] = yhat
            q_ref[c] = Q.astype(q_ref.dtype)
            n_ref[c] = Nt
            wc_ref[c] = wc

    @pl.when(i == n_blocks)
    def _():
        S = s_ref[...]
        for c in range(n_chunks):
            S = recurrence_step(c, S)


def _rwkv_scan(r, lw, k, v, a, b, g, bg, lnw, lnb, tm):
    B, S, W = r.shape
    n_pairs = W // LANES
    n_blocks = S // tm
    n_chunks = tm // SCAN_CHUNK
    cur = pl.BlockSpec((1, tm, LANES), lambda bb, p, i: (bb, jnp.minimum(i, n_blocks - 1), p))
    prev = pl.BlockSpec((1, tm, LANES), lambda bb, p, i: (bb, jnp.maximum(i - 1, 0), p))
    vec = pl.BlockSpec((1, LANES), lambda bb, p, i: (0, p))
    return pl.pallas_call(
        _rwkv_scan_kernel,
        grid=(B, n_pairs, n_blocks + 1),
        in_specs=[cur] * 6 + [prev, prev, vec, vec],
        out_specs=prev,
        out_shape=jax.ShapeDtypeStruct((B, S, W), BF16),
        scratch_shapes=[pltpu.VMEM((LANES, LANES), F32),
                        pltpu.VMEM((n_chunks, SCAN_CHUNK, LANES), BF16),
                        pltpu.VMEM((n_chunks, SCAN_CHUNK, LANES), F32),
                        pltpu.VMEM((n_chunks, LANES, LANES), BF16),
                        pltpu.VMEM((n_chunks, LANES, LANES), F32),
                        pltpu.VMEM((n_chunks, 1, LANES), F32)],
        compiler_params=_cparams("parallel", "parallel", "arbitrary"),
        name="rwkv_scan",
    )(r, lw, k, v, a, b, g, bg, lnw.reshape(1, W).astype(F32), lnb.reshape(1, W).astype(F32))


def _post_kernel(x_ref, oa_ref, yr_ref, gates_ref, p_ref, woa_ref, wor_ref, wout_ref, nffn_ref,
                 wg_ref, wu_ref, wd_ref, nple_ref, wple_ref, wpg_ref, nfin_ref, o_ref, *, final, ff_chunk):
    D = x_ref.shape[1]
    ya = jnp.dot(oa_ref[...], woa_ref[...], preferred_element_type=F32)
    yr = jnp.dot(yr_ref[...], wor_ref[...], preferred_element_type=F32)
    m = gates_ref[:, 0:D].astype(F32) * ya + gates_ref[:, D:2 * D].astype(F32) * yr
    x = x_ref[...] + _dot(m, wout_ref[...])

    h2 = _rms(x, nffn_ref[...], NORM_EPS).astype(BF16)
    d_ff = wg_ref.shape[1]
    acc = jnp.zeros_like(x)
    for c0 in range(0, d_ff, ff_chunk):
        gt = jnp.dot(h2, wg_ref[:, c0:c0 + ff_chunk], preferred_element_type=F32)
        up = jnp.dot(h2, wu_ref[:, c0:c0 + ff_chunk], preferred_element_type=F32)
        act = gt * jax.nn.sigmoid(gt) * up
        acc = acc + jnp.dot(act.astype(BF16), wd_ref[c0:c0 + ff_chunk, :], preferred_element_type=F32)
    x = x + acc

    e = _dot(p_ref[...], wple_ref[...])
    gp = jax.nn.sigmoid(_dot(_rms(x, nple_ref[...], NORM_EPS), wpg_ref[...]))
    x = x + gp * e
    if final:
        x = _rms(x, nfin_ref[...], NORM_EPS)
    o_ref[...] = x


def _post(x2d, oa, yr, gates, p2d, woa, wor, wout, nffn, wg, wu, wd, nple, wple, wpg, nfin, final, tm):
    T, D = x2d.shape
    d_ff = wg.shape[1]
    ff_chunk = 256 if d_ff % 256 == 0 else LANES
    tok = lambda w: pl.BlockSpec((tm, w), lambda i: (i, 0))
    row = lambda v: v.reshape(1, -1).astype(F32)
    bf = lambda w: w.astype(BF16)
    return pl.pallas_call(
        functools.partial(_post_kernel, final=final, ff_chunk=ff_chunk),
        grid=(T // tm,),
        in_specs=[tok(D), tok(oa.shape[1]), tok(yr.shape[1]), tok(gates.shape[1]), tok(p2d.shape[1]),
                  _const_spec(woa.shape), _const_spec(wor.shape), _const_spec(wout.shape),
                  _const_spec((1, D)), _const_spec(wg.shape), _const_spec(wu.shape), _const_spec(wd.shape),
                  _const_spec((1, D)), _const_spec(wple.shape), _const_spec(wpg.shape), _const_spec((1, D))],
        out_specs=tok(D),
        out_shape=jax.ShapeDtypeStruct((T, D), F32),
        compiler_params=_cparams("parallel"),
        name="post",
    )(x2d, oa, yr, gates, p2d, bf(woa), bf(wor), bf(wout), row(nffn), bf(wg), bf(wu), bf(wd),
      row(nple), bf(wple), bf(wpg), row(nfin))


def _tile(n, pref):
    t = min(n, pref)
    assert n % t == 0, (n, t)
    return t


def kernel(x, p, rel_bias, norm_mix, w_in, lam_q1, lam_k1, lam_q2, lam_k2, attn_subln, rwkv_mu, rwkv_w0, rwkv_w2, rwkv_a0, rwkv_a2, rwkv_g2, rwkv_kk, rwkv_ka, rwkv_rk, rwkv_lnx_w, rwkv_lnx_b, w_out_attn, w_out_rwkv, w_out, norm_ffn, w_ffn_gate, w_ffn_up, w_ffn_down, norm_ple, w_ple, w_ple_gate, norm_final):
    B, S, D = x.shape
    depth = w_in.shape[0]
    T = B * S
    t_attn = _tile(S, ATTN_TILE)
    assert t_attn >= 113
    tm_tok = _tile(T, 512)
    tm_prep = _tile(S, 256)
    tm_scan = _tile(S, 16 * SCAN_CHUNK)
    assert tm_scan % SCAN_CHUNK == 0

    bias_tiles = _bias_tiles(rel_bias, t_attn)
    x2d = x.reshape(T, D)
    for i in range(depth):
        qkv, zr, gates = _inproj(x2d, norm_mix[i], w_in[i].astype(BF16), tm_tok)

        lam_init = 0.8 - 0.6 * math.exp(-0.3 * i)
        lam = (jnp.exp(jnp.sum(lam_q1[i] * lam_k1[i])) - jnp.exp(jnp.sum(lam_q2[i] * lam_k2[i])) + lam_init)
        scal = jnp.stack([lam.astype(F32), jnp.asarray(1.0 - lam_init, F32)])
        oa = _attention(qkv.reshape(B, S, -1), bias_tiles, scal, attn_subln[i], t_attn,
                        max(1, min(ATTN_FAR_TILES, S // t_attn)))

        r, lw, k, v, a, b, g, bg = _rwkv_prep(zr.reshape(B, S, -1), rwkv_mu[i], rwkv_w0[i], rwkv_w2[i],
                                              rwkv_a0[i], rwkv_a2[i], rwkv_g2[i], rwkv_kk[i], rwkv_ka[i],
                                              rwkv_rk[i].reshape(-1), tm_prep)
        yr = _rwkv_scan(r, lw, k, v, a, b, g, bg, rwkv_lnx_w[i], rwkv_lnx_b[i], tm_scan)

        x2d = _post(x2d, oa.reshape(T, -1), yr.reshape(T, -1), gates, p[i].reshape(T, -1),
                    w_out_attn[i], w_out_rwkv[i], w_out[i], norm_ffn[i], w_ffn_gate[i], w_ffn_up[i],
                    w_ffn_down[i], norm_ple[i], w_ple[i], w_ple_gate[i], norm_final,
                    final=(i == depth - 1), tm=tm_tok)
    return x2d.reshape(B, S, D)
```

```python
import functools
import math

import jax
import jax.numpy as jnp
from jax import lax
from jax.experimental import pallas as pl
from jax.experimental.pallas import tpu as pltpu

F32 = jnp.float32
BF16 = jnp.bfloat16

N_ATTN_HEADS = 4
ATTN_HALF_DIM = 64
ATTN_V_DIM = 128
ATTN_WIDTH = 512
N_RWKV_HEADS = 8
RWKV_HEAD_DIM = 64
RWKV_WIDTH = 512
DECAY_LORA = 64
AAA_LORA = 64
GATE_LORA = 128
RWKV_COLS = 3 * RWKV_WIDTH + DECAY_LORA + AAA_LORA + GATE_LORA
NUM_BUCKETS = 32
MAX_DISTANCE = 128
NORM_EPS = 1e-6
SUBLN_EPS = 1e-5
GN_EPS = 64e-5

LANES = 128
BF16_SUBLANES = 16
VMEM_LIMIT_BYTES = 56 * 1024 * 1024
ATTN_TILE = 512
ATTN_FAR_TILES = 4
ATTN_ROW_SPLIT = 2
LOG2_E = math.log2(math.e)
SCAN_CHUNK = 64
MASK_VALUE = -1e30


def _cparams(*sem):
    return pltpu.CompilerParams(dimension_semantics=sem, vmem_limit_bytes=VMEM_LIMIT_BYTES)


def _const_spec(shape):
    nd = len(shape)
    return pl.BlockSpec(shape, lambda *_: (0,) * nd, pipeline_mode=pl.Buffered(1))


def _dot(a, b):
    return jnp.dot(a.astype(BF16), b.astype(BF16), preferred_element_type=F32)


def _dot_nt(a, b):
    return lax.dot_general(a.astype(BF16), b.astype(BF16), (((1,), (1,)), ((), ())),
                           preferred_element_type=F32)


def _dot_tn(a, b):
    return lax.dot_general(a.astype(BF16), b.astype(BF16), (((0,), (0,)), ((), ())),
                           preferred_element_type=F32)


def _dot_hilo(a_exact_bf16, x):
    hi = x.astype(BF16)
    lo = (x - hi.astype(F32)).astype(BF16)
    return (jnp.dot(a_exact_bf16, hi, preferred_element_type=F32)
            + jnp.dot(a_exact_bf16, lo, preferred_element_type=F32))


def _rms(x, g, eps):
    return x * lax.rsqrt(jnp.mean(x * x, axis=-1, keepdims=True) + eps) * g


def _inproj_kernel(x_ref, g_ref, w_ref, qkv_ref, zr_ref, gates_ref):
    h = _rms(x_ref[...], g_ref[...], NORM_EPS).astype(BF16)
    scale = ATTN_HALF_DIM ** -0.5 * LOG2_E
    nq = 3 * ATTN_WIDTH
    for c0 in range(0, nq, 512):
        z = jnp.dot(h, w_ref[:, c0:c0 + 512], preferred_element_type=F32)
        if c0 < ATTN_WIDTH:
            z = z * scale
        qkv_ref[:, c0:c0 + 512] = z.astype(BF16)
    c0 = nq
    while c0 < nq + RWKV_COLS:
        w = min(512, nq + RWKV_COLS - c0)
        zr_ref[:, c0 - nq:c0 - nq + w] = jnp.dot(h, w_ref[:, c0:c0 + w],
                                                 preferred_element_type=F32).astype(zr_ref.dtype)
        c0 += w
    base = nq + RWKV_COLS
    for c0 in range(0, gates_ref.shape[1], 512):
        z = jnp.dot(h, w_ref[:, base + c0:base + c0 + 512], preferred_element_type=F32)
        gates_ref[:, c0:c0 + 512] = jax.nn.sigmoid(z).astype(BF16)


def _inproj(x2d, g, w_bf16, tm):
    T, D = x2d.shape
    ncols = w_bf16.shape[1]
    ngate = ncols - 3 * ATTN_WIDTH - RWKV_COLS
    return pl.pallas_call(
        _inproj_kernel,
        grid=(T // tm,),
        in_specs=[pl.BlockSpec((tm, D), lambda i: (i, 0)),
                  _const_spec((1, D)),
                  _const_spec((D, ncols))],
        out_specs=[pl.BlockSpec((tm, 3 * ATTN_WIDTH), lambda i: (i, 0)),
                   pl.BlockSpec((tm, RWKV_COLS), lambda i: (i, 0)),
                   pl.BlockSpec((tm, ngate), lambda i: (i, 0))],
        out_shape=[jax.ShapeDtypeStruct((T, 3 * ATTN_WIDTH), BF16),
                   jax.ShapeDtypeStruct((T, RWKV_COLS), BF16),
                   jax.ShapeDtypeStruct((T, ngate), BF16)],
        compiler_params=_cparams("parallel"),
        name="inproj",
    )(x2d, g.reshape(1, D), w_bf16)


def _bias_tiles_kernel(rb_ref, out_ref):
    hc = pl.program_id(0)
    t = out_ref.shape[-2]
    rows = lax.broadcasted_iota(jnp.int32, (t, 2 * t), 0)
    cols = lax.broadcasted_iota(jnp.int32, (t, 2 * t), 1)
    max_exact = NUM_BUCKETS // 2
    n_hc = 2 * N_ATTN_HEADS
    far = rb_ref[(NUM_BUCKETS - 1) * n_hc + hc]
    dist = t + rows - cols
    n = jnp.maximum(dist, 0)
    nf = jnp.maximum(n, max_exact).astype(F32)
    large = max_exact + (jnp.log(nf / max_exact) / math.log(MAX_DISTANCE / max_exact)
                         * (NUM_BUCKETS - max_exact)).astype(jnp.int32)
    large = jnp.minimum(large, NUM_BUCKETS - 1)
    bucket = jnp.where(n < max_exact, n, large)
    tile = jnp.zeros((t, 2 * t), F32)
    for b in range(NUM_BUCKETS):
        tile = jnp.where(bucket == b, rb_ref[b * n_hc + hc], tile)
    tile = (tile - far) * LOG2_E
    out_ref[0] = jnp.where(dist >= 0, tile, MASK_VALUE)


def _bias_tiles(rel_bias, t):
    n_hc = 2 * N_ATTN_HEADS
    return pl.pallas_call(
        _bias_tiles_kernel,
        grid=(n_hc,),
        in_specs=[pl.BlockSpec(memory_space=pltpu.SMEM)],
        out_specs=pl.BlockSpec((1, t, 2 * t), lambda i: (i, 0, 0)),
        out_shape=jax.ShapeDtypeStruct((n_hc, t, 2 * t), F32),
        compiler_params=_cparams("parallel"),
        name="bias_tiles",
    )(rel_bias.reshape(-1).astype(F32))


def _attn_kernel(scal_ref, q_ref, k_ref, v_ref, bias_ref, g_ref, o_ref, m_ref, acc_ref, v1_ref, *,
                 far_tiles, row_split):
    t = q_ref.shape[1]
    tr = t // row_split
    qi = pl.program_id(2)

    @pl.when(qi == 0)
    def _():
        v1_ref[:, 0:ATTN_V_DIM] = v_ref[0]
        v1_ref[:, ATTN_V_DIM:] = jnp.ones((v1_ref.shape[0], LANES), BF16)

    lane = lax.broadcasted_iota(jnp.int32, (1, LANES), 1)
    q = q_ref[0]
    zero = jnp.zeros_like(q)
    qm = (jnp.where(lane < ATTN_HALF_DIM, q, zero), jnp.where(lane >= ATTN_HALF_DIM, q, zero))

    m_ref[...] = jnp.full(m_ref.shape, MASK_VALUE, F32)
    acc_ref[...] = jnp.zeros(acc_ref.shape, F32)

    streams = [(c, r) for c in range(2) for r in range(row_split)]

    def step(start, width, kind):
        kb = k_ref[0, pl.ds(start, width), :]
        vb = v1_ref[pl.ds(start, width), :]
        def scores(c, r):
            w = width if kind is None else width - t + (r + 1) * tr
            s = lax.dot_general(qm[c][r * tr:(r + 1) * tr], kb[0:w], (((1,), (1,)), ((), ())),
                                preferred_element_type=F32)
            if kind is not None:
                b0 = 2 * t - width
                s = s + bias_ref[0, c, r * tr:(r + 1) * tr, b0:b0 + w]
            return s

        def softmax_pv(c, r, s):
            rows = slice(r * tr, (r + 1) * tr)
            w = s.shape[1]
            m_prev = m_ref[c, rows, :]
            m_new = jnp.maximum(m_prev, jnp.max(s, axis=-1, keepdims=True))
            alpha = jnp.exp2(m_prev - m_new)
            p = jnp.exp2(s - jnp.concatenate([m_new] * (w // LANES), axis=1))
            acc_ref[c, rows, :] = (jnp.concatenate([alpha, alpha], axis=1) * acc_ref[c, rows, :]
                                   + jnp.dot(p.astype(BF16), vb[0:w], preferred_element_type=F32))
            m_ref[c, rows, :] = m_new

        pending = [scores(c, r) for c, r in streams]
        for (c, r), s in zip(streams, pending):
            softmax_pv(c, r, s)

    n_far = jnp.maximum(qi - 1, 0)
    far_w = far_tiles * t

    def far_body(j, carry):
        step(pl.multiple_of(j * far_w, far_w), far_w, None)
        return carry

    n_wide = n_far // far_tiles
    lax.fori_loop(0, n_wide, far_body, 0)
    pos = n_wide * far_tiles
    rem = n_far - pos
    w_tiles = far_tiles // 2
    while w_tiles >= 1:
        take = rem >= w_tiles

        @pl.when(take)
        def _(pos=pos, w_tiles=w_tiles):
            step(pl.multiple_of(pos * t, t), w_tiles * t, None)

        pos = pos + jnp.where(take, w_tiles, 0)
        rem = rem - jnp.where(take, w_tiles, 0)
        w_tiles //= 2

    if k_ref.shape[1] >= 2 * t:
        @pl.when(qi >= 1)
        def _():
            step(pl.multiple_of((qi - 1) * t, t), 2 * t, "near")

    @pl.when(qi == 0)
    def _():
        step(0, t, "first")

    lam = scal_ref[0]
    out_scale = scal_ref[1]
    dv = ATTN_V_DIM
    o = acc_ref[0, :, 0:dv] / acc_ref[0, :, dv:] - lam * (acc_ref[1, :, 0:dv] / acc_ref[1, :, dv:])
    o_ref[0] = (_rms(o, g_ref[...], SUBLN_EPS) * out_scale).astype(o_ref.dtype)


def _attention(qkv, bias_tiles, scal, subln_g, t, far_tiles):
    B, S, _ = qkv.shape
    H = N_ATTN_HEADS
    return pl.pallas_call(
        functools.partial(_attn_kernel, far_tiles=far_tiles, row_split=ATTN_ROW_SPLIT),
        grid=(B, H, S // t),
        in_specs=[pl.BlockSpec(memory_space=pltpu.SMEM),
                  pl.BlockSpec((1, t, LANES), lambda b, h, i: (b, i, h)),
                  pl.BlockSpec((1, S, LANES), lambda b, h, i: (b, 0, H + h)),
                  pl.BlockSpec((1, S, LANES), lambda b, h, i: (b, 0, 2 * H + h)),
                  pl.BlockSpec((1, 2, t, 2 * t), lambda b, h, i: (h, 0, 0, 0)),
                  _const_spec((1, ATTN_V_DIM))],
        out_specs=pl.BlockSpec((1, t, LANES), lambda b, h, i: (b, i, h)),
        out_shape=jax.ShapeDtypeStruct((B, S, ATTN_WIDTH), BF16),
        scratch_shapes=[pltpu.VMEM((2, t, LANES), F32),
                        pltpu.VMEM((2, t, ATTN_V_DIM + LANES), F32),
                        pltpu.VMEM((S, ATTN_V_DIM + LANES), BF16)],
        compiler_params=_cparams("parallel", "parallel", "arbitrary"),
        name="diff_attention",
    )(scal, qkv, qkv, qkv, bias_tiles.reshape(H, 2, t, 2 * t), subln_g.reshape(1, ATTN_V_DIM))


def _rwkv_prep_kernel(z_ref, zp_ref, mu_ref, w0_ref, w2_ref, a0_ref, a2_ref, g2_ref, kk_ref, ka_ref,
                      rk_ref, seg_ref,
                      r_ref, lw_ref, k_ref, v_ref, a_ref, b_ref, g_ref, bg_ref):
    i = pl.program_id(1)
    z = z_ref[0].astype(F32)
    tm = z.shape[0]
    W = RWKV_WIDTH
    last = zp_ref.shape[1] - 1
    prev_row = zp_ref[0, last:last + 1, :].astype(F32) * (i > 0).astype(F32)
    row = lax.broadcasted_iota(jnp.int32, (tm, 1), 0)
    prev = jnp.where(row == 0, prev_row, pltpu.roll(z, 1, 0))
    zs = z + (prev - z) * mu_ref[...]
    r = zs[:, 0:W]
    kr = zs[:, W:2 * W]
    vr = zs[:, 2 * W:3 * W]
    xwa = zs[:, 3 * W:3 * W + DECAY_LORA + AAA_LORA]
    xg = zs[:, 3 * W + DECAY_LORA + AAA_LORA:]

    dw = w0_ref[...] + _dot(jnp.tanh(xwa), w2_ref[...])
    softplus = jnp.maximum(-dw, 0.0) + jnp.log1p(jnp.exp(-jnp.abs(dw)))
    lw_ref[0] = -jnp.exp(-softplus - 0.5)
    asig = jax.nn.sigmoid(a0_ref[...] + _dot(xwa, a2_ref[...]))
    g = _dot(jax.nn.sigmoid(xg), g2_ref[...])

    seg = seg_ref[...]
    kk = kr * kk_ref[...]
    norm = jnp.sqrt(_dot_hilo_rhs_exact(kk * kk, seg))
    kkn = kk / jnp.maximum(norm, 1e-12)
    kmod = kr * (1.0 + (asig - 1.0) * ka_ref[...])
    bonus = _dot_hilo_rhs_exact(r * kmod * rk_ref[...], seg) * vr

    r_ref[0] = r.astype(r_ref.dtype)
    k_ref[0] = kmod.astype(k_ref.dtype)
    v_ref[0] = vr.astype(v_ref.dtype)
    a_ref[0] = (-kkn).astype(a_ref.dtype)
    b_ref[0] = (kkn * asig).astype(b_ref.dtype)
    g_ref[0] = g.astype(g_ref.dtype)
    bg_ref[0] = (bonus * g).astype(bg_ref.dtype)


def _dot_hilo_rhs_exact(x, seg):
    hi = x.astype(BF16)
    lo = (x - hi.astype(F32)).astype(BF16)
    return (jnp.dot(hi, seg, preferred_element_type=F32) + jnp.dot(lo, seg, preferred_element_type=F32))


def _rwkv_prep(zr, mu, w0, w2, a0, a2, g2, kkp, ka, rk, tm):
    B, S, C = zr.shape
    W = RWKV_WIDTH
    w2p = jnp.concatenate([w2, jnp.zeros_like(w2)], axis=0).astype(BF16)
    a2p = jnp.concatenate([jnp.zeros_like(a2), a2], axis=0).astype(BF16)
    head = jnp.arange(W) // RWKV_HEAD_DIM
    seg = (head[:, None] == head[None, :]).astype(BF16)
    row = lambda v: v.reshape(1, -1).astype(F32)
    sub = BF16_SUBLANES
    nblk = tm // sub
    out_spec = pl.BlockSpec((1, tm, W), lambda b, i: (b, i, 0))
    return pl.pallas_call(
        _rwkv_prep_kernel,
        grid=(B, S // tm),
        in_specs=[pl.BlockSpec((1, tm, C), lambda b, i: (b, i, 0)),
                  pl.BlockSpec((1, sub, C), lambda b, i: (b, jnp.maximum(i * nblk - 1, 0), 0)),
                  _const_spec((1, C)), _const_spec((1, W)), _const_spec((LANES, W)),
                  _const_spec((1, W)), _const_spec((LANES, W)), _const_spec((GATE_LORA, W)),
                  _const_spec((1, W)), _const_spec((1, W)), _const_spec((1, W)),
                  _const_spec((W, W))],
        out_specs=[out_spec] * 8,
        out_shape=[jax.ShapeDtypeStruct((B, S, W), F32 if n == 1 else BF16) for n in range(8)],
        compiler_params=_cparams("parallel", "parallel"),
        name="rwkv_prep",
    )(zr, zr, row(mu), row(w0), w2p, row(a0), a2p, g2.astype(BF16), row(kkp), row(ka), row(rk), seg)


def _scan_local(chunks, consts, tick):
    C = SCAN_CHUNK
    n = len(chunks)
    tril, strict2, incl2, lane_lo, lane_hi, bd = consts
    zc = jnp.zeros((C, LANES), F32)
    cat0 = lambda *xs: jnp.concatenate(xs, axis=0)
    cat1 = lambda *xs: jnp.concatenate(xs, axis=1)

    L_all = _dot_hilo(tril, cat1(*[ch[1] for ch in chunks]))
    pre = []
    for i, (r, lw, k, v, a, b) in enumerate(chunks):
        L = L_all[:, i * LANES:(i + 1) * LANES]
        winv = jnp.exp(-L)
        l_end = L[C - 1:C, :]
        wend = jnp.exp(l_end - L)
        rt = r * jnp.exp(L)
        at = a * jnp.exp(L - lw)
        bk = cat0(b * winv, k * winv)
        bk_end = cat0(b * wend, k * wend)
        pre.append((rt, at, bk, bk_end, jnp.exp(l_end)))
    tick()

    outs = []
    for rt, at, bk, _, _ in pre:
        ar = cat0(at, rt)
        zero = jnp.zeros_like(ar)
        outs.append(_dot_nt(cat0(jnp.where(lane_lo, ar, zero), jnp.where(lane_hi, ar, zero)), bk))
    tops = [(jnp.where(strict2, o[0:C], 0.0), jnp.where(strict2, o[2 * C:3 * C], 0.0)) for o in outs]
    bots = [cat0(jnp.where(incl2, o[C:2 * C], 0.0), jnp.where(incl2, o[3 * C:4 * C], 0.0)) for o in outs]
    tick()

    aak_v = [_dot(cat0(jnp.where(lane_hi, t0, zc), jnp.where(lane_hi, t1, zc)), cat0(ch[3], ch[3]))
             for (t0, t1), ch in zip(tops, chunks)]
    tick()
    A = [cat0(jnp.where(lane_lo, t0, zc), jnp.where(lane_hi, pltpu.roll(t1, C, 1), zc)) for t0, t1 in tops]
    Z = [cat1(cat0(p[1], p[1]), av) for p, av in zip(pre, aak_v)]
    n_steps = int(math.log2(C))
    for s in range(n_steps):
        Z = [z + _dot(a_, z) for a_, z in zip(A, Z)]
        tick()
        if s + 1 < n_steps:
            A = [_dot(a_, a_) for a_ in A]
            tick()
    ahat = [jnp.where(lane_lo, z[0:C, 0:LANES], z[C:2 * C, 0:LANES]) for z in Z]
    vhat = [jnp.where(lane_lo, z[0:C, LANES:], z[C:2 * C, LANES:]) for z in Z]

    yr2 = [_dot(bot, cat0(cat1(vh, ah), cat1(ch[3], zc)))
           for bot, vh, ah, ch in zip(bots, vhat, ahat, chunks)]
    tick()
    qn = [_dot_tn(cat0(cat1(ah, vh), cat1(zc, ch[3])), p[3])
          for ah, vh, ch, p in zip(ahat, vhat, chunks, pre)]
    tick()
    res = []
    for i in range(n):
        yhat = jnp.where(lane_lo, yr2[i][0:C, 0:LANES], yr2[i][C:2 * C, 0:LANES])
        rhat = pre[i][0] + jnp.where(lane_lo, yr2[i][0:C, LANES:], yr2[i][C:2 * C, LANES:])
        Q = jnp.where(bd, qn[i][0:LANES], 0.0)
        Nt = jnp.where(bd, qn[i][LANES:], 0.0)
        res.append((rhat, yhat, Q, Nt, pre[i][4]))
    return res


SCAN_LOCAL_STAGES = 5 + 2 * int(math.log2(SCAN_CHUNK)) - 1


def _rwkv_scan_kernel(r_ref, lw_ref, k_ref, v_ref, a_ref, b_ref, g_ref, bg_ref, lnw_ref, lnb_ref,
                      o_ref, s_ref, rh_ref, yh_ref, q_ref, n_ref, wc_ref):
    C = SCAN_CHUNK
    n_chunks = r_ref.shape[1] // C
    i = pl.program_id(2)
    n_blocks = pl.num_programs(2) - 1

    @pl.when(i == 0)
    def _():
        s_ref[...] = jnp.zeros(s_ref.shape, F32)
        rh_ref[...] = jnp.zeros(rh_ref.shape, rh_ref.dtype)
        yh_ref[...] = jnp.zeros(yh_ref.shape, F32)
        q_ref[...] = jnp.zeros(q_ref.shape, q_ref.dtype)
        n_ref[...] = jnp.zeros(n_ref.shape, F32)
        wc_ref[...] = jnp.zeros(wc_ref.shape, F32)

    ri = lax.broadcasted_iota(jnp.int32, (C, 2 * C), 0)
    ci = lax.broadcasted_iota(jnp.int32, (C, 2 * C), 1) % C
    strict2 = ci < ri
    incl2 = ci <= ri
    lane = lax.broadcasted_iota(jnp.int32, (1, LANES), 1)
    lane_lo = lane < RWKV_HEAD_DIM
    lane_hi = lane >= RWKV_HEAD_DIM
    r128 = lax.broadcasted_iota(jnp.int32, (LANES, LANES), 0)
    c128 = lax.broadcasted_iota(jnp.int32, (LANES, LANES), 1)
    bd = (r128 < RWKV_HEAD_DIM) == (c128 < RWKV_HEAD_DIM)
    tril = (lax.broadcasted_iota(jnp.int32, (C, C), 1) <= lax.broadcasted_iota(jnp.int32, (C, C), 0)).astype(BF16)
    consts = (tril, strict2, incl2, lane_lo, lane_hi, bd)
    inv_n = 1.0 / RWKV_HEAD_DIM

    def head_sums(x):
        lo = jnp.sum(jnp.where(lane_lo, x, 0.0), axis=-1, keepdims=True)
        hi = jnp.sum(jnp.where(lane_hi, x, 0.0), axis=-1, keepdims=True)
        return jnp.where(lane_lo, lo, hi)

    def recurrence_step(c, S):
        sl = slice(c * C, (c + 1) * C)
        y = _dot_nt(rh_ref[c], S) + yh_ref[c]
        S = S * wc_ref[c] + _dot(S, q_ref[c]) + n_ref[c]
        mean = head_sums(y) * inv_n
        d = y - mean
        var = head_sums(d * d) * inv_n
        yn = d * lax.rsqrt(var + GN_EPS) * lnw_ref[...] + lnb_ref[...]
        o_ref[0, sl, :] = (yn * g_ref[0, sl, :].astype(F32) + bg_ref[0, sl, :].astype(F32)).astype(o_ref.dtype)
        return S

    @pl.when(i < n_blocks)
    def _():
        state = [s_ref[...]]
        done = [0]
        ticks = [0]

        def tick():
            ticks[0] += 1
            target = min(n_chunks, (ticks[0] * n_chunks + SCAN_LOCAL_STAGES - 1) // SCAN_LOCAL_STAGES)
            while done[0] < target:
                state[0] = recurrence_step(done[0], state[0])
                done[0] += 1

        chunks = []
        for c in range(n_chunks):
            sl = slice(c * C, (c + 1) * C)
            chunks.append(tuple(ref[0, sl, :].astype(F32)
                                for ref in (r_ref, lw_ref, k_ref, v_ref, a_ref, b_ref)))
        local = _scan_local(chunks, consts, tick)
        assert done[0] == n_chunks
        s_ref[...] = state[0]
        for c in range(n_chunks):
            rhat, yhat, Q, Nt, wc = local[c]
            rh_ref[c] = rhat.astype(rh_ref.dtype)
            yh_ref[c] = yhat
            q_ref[c] = Q.astype(q_ref.dtype)
            n_ref[c] = Nt
            wc_ref[c] = wc

    @pl.when(i == n_blocks)
    def _():
        S = s_ref[...]
        for c in range(n_chunks):
            S = recurrence_step(c, S)


def _rwkv_scan(r, lw, k, v, a, b, g, bg, lnw, lnb, tm):
    B, S, W = r.shape
    n_pairs = W // LANES
    n_blocks = S // tm
    n_chunks = tm // SCAN_CHUNK
    cur = pl.BlockSpec((1, tm, LANES), lambda bb, p, i: (bb, jnp.minimum(i, n_blocks - 1), p))
    prev = pl.BlockSpec((1, tm, LANES), lambda bb, p, i: (bb, jnp.maximum(i - 1, 0), p))
    vec = pl.BlockSpec((1, LANES), lambda bb, p, i: (0, p))
    return pl.pallas_call(
        _rwkv_scan_kernel,
        grid=(B, n_pairs, n_blocks + 1),
        in_specs=[cur] * 6 + [prev, prev, vec, vec],
        out_specs=prev,
        out_shape=jax.ShapeDtypeStruct((B, S, W), BF16),
        scratch_shapes=[pltpu.VMEM((LANES, LANES), F32),
                        pltpu.VMEM((n_chunks, SCAN_CHUNK, LANES), BF16),
                        pltpu.VMEM((n_chunks, SCAN_CHUNK, LANES), F32),
                        pltpu.VMEM((n_chunks, LANES, LANES), BF16),
                        pltpu.VMEM((n_chunks, LANES, LANES), F32),
                        pltpu.VMEM((n_chunks, 1, LANES), F32)],
        compiler_params=_cparams("parallel", "parallel", "arbitrary"),
        name="rwkv_scan",
    )(r, lw, k, v, a, b, g, bg, lnw.reshape(1, W).astype(F32), lnb.reshape(1, W).astype(F32))


def _post_kernel(x_ref, oa_ref, yr_ref, gates_ref, p_ref, woa_ref, wor_ref, wout_ref, nffn_ref,
                 wg_ref, wu_ref, wd_ref, nple_ref, wple_ref, wpg_ref, nfin_ref, o_ref, *, final, ff_chunk):
    D = x_ref.shape[1]
    ya = jnp.dot(oa_ref[...], woa_ref[...], preferred_element_type=F32)
    yr = jnp.dot(yr_ref[...], wor_ref[...], preferred_element_type=F32)
    m = gates_ref[:, 0:D].astype(F32) * ya + gates_ref[:, D:2 * D].astype(F32) * yr
    x = x_ref[...] + _dot(m, wout_ref[...])

    h2 = _rms(x, nffn_ref[...], NORM_EPS).astype(BF16)
    d_ff = wg_ref.shape[1]
    acc = jnp.zeros_like(x)
    for c0 in range(0, d_ff, ff_chunk):
        gt = jnp.dot(h2, wg_ref[:, c0:c0 + ff_chunk], preferred_element_type=F32)
        up = jnp.dot(h2, wu_ref[:, c0:c0 + ff_chunk], preferred_element_type=F32)
        act = gt * jax.nn.sigmoid(gt) * up
        acc = acc + jnp.dot(act.astype(BF16), wd_ref[c0:c0 + ff_chunk, :], preferred_element_type=F32)
    x = x + acc

    e = _dot(p_ref[...], wple_ref[...])
    gp = jax.nn.sigmoid(_dot(_rms(x, nple_ref[...], NORM_EPS), wpg_ref[...]))
    x = x + gp * e
    if final:
        x = _rms(x, nfin_ref[...], NORM_EPS)
    o_ref[...] = x


def _post(x2d, oa, yr, gates, p2d, woa, wor, wout, nffn, wg, wu, wd, nple, wple, wpg, nfin, final, tm):
    T, D = x2d.shape
    d_ff = wg.shape[1]
    ff_chunk = 256 if d_ff % 256 == 0 else LANES
    tok = lambda w: pl.BlockSpec((tm, w), lambda i: (i, 0))
    row = lambda v: v.reshape(1, -1).astype(F32)
    bf = lambda w: w.astype(BF16)
    return pl.pallas_call(
        functools.partial(_post_kernel, final=final, ff_chunk=ff_chunk),
        grid=(T // tm,),
        in_specs=[tok(D), tok(oa.shape[1]), tok(yr.shape[1]), tok(gates.shape[1]), tok(p2d.shape[1]),
                  _const_spec(woa.shape), _const_spec(wor.shape), _const_spec(wout.shape),
                  _const_spec((1, D)), _const_spec(wg.shape), _const_spec(wu.shape), _const_spec(wd.shape),
                  _const_spec((1, D)), _const_spec(wple.shape), _const_spec(wpg.shape), _const_spec((1, D))],
        out_specs=tok(D),
        out_shape=jax.ShapeDtypeStruct((T, D), F32),
        compiler_params=_cparams("parallel"),
        name="post",
    )(x2d, oa, yr, gates, p2d, bf(woa), bf(wor), bf(wout), row(nffn), bf(wg), bf(wu), bf(wd),
      row(nple), bf(wple), bf(wpg), row(nfin))


def _tile(n, pref):
    t = min(n, pref)
    assert n % t == 0, (n, t)
    return t


def kernel(x, p, rel_bias, norm_mix, w_in, lam_q1, lam_k1, lam_q2, lam_k2, attn_subln, rwkv_mu, rwkv_w0, rwkv_w2, rwkv_a0, rwkv_a2, rwkv_g2, rwkv_kk, rwkv_ka, rwkv_rk, rwkv_lnx_w, rwkv_lnx_b, w_out_attn, w_out_rwkv, w_out, norm_ffn, w_ffn_gate, w_ffn_up, w_ffn_down, norm_ple, w_ple, w_ple_gate, norm_final):
    B, S, D = x.shape
    depth = w_in.shape[0]
    T = B * S
    t_attn = _tile(S, ATTN_TILE)
    assert t_attn >= 113
    tm_tok = _tile(T, 512)
    tm_prep = _tile(S, 256)
    tm_scan = _tile(S, 16 * SCAN_CHUNK)
    assert tm_scan % SCAN_CHUNK == 0

    bias_tiles = _bias_tiles(rel_bias, t_attn)
    x2d = x.reshape(T, D)
    for i in range(depth):
        qkv, zr, gates = _inproj(x2d, norm_mix[i], w_in[i].astype(BF16), tm_tok)

        lam_init = 0.8 - 0.6 * math.exp(-0.3 * i)
        lam = (jnp.exp(jnp.sum(lam_q1[i] * lam_k1[i])) - jnp.exp(jnp.sum(lam_q2[i] * lam_k2[i])) + lam_init)
        scal = jnp.stack([lam.astype(F32), jnp.asarray(1.0 - lam_init, F32)])
        oa = _attention(qkv.reshape(B, S, -1), bias_tiles, scal, attn_subln[i], t_attn,
                        max(1, min(ATTN_FAR_TILES, S // t_attn)))

        r, lw, k, v, a, b, g, bg = _rwkv_prep(zr.reshape(B, S, -1), rwkv_mu[i], rwkv_w0[i], rwkv_w2[i],
                                              rwkv_a0[i], rwkv_a2[i], rwkv_g2[i], rwkv_kk[i], rwkv_ka[i],
                                              rwkv_rk[i].reshape(-1), tm_prep)
        yr = _rwkv_scan(r, lw, k, v, a, b, g, bg, rwkv_lnx_w[i], rwkv_lnx_b[i], tm_scan)

        x2d = _post(x2d, oa.reshape(T, -1), yr.reshape(T, -1), gates, p[i].reshape(T, -1),
                    w_out_attn[i], w_out_rwkv[i], w_out[i], norm_ffn[i], w_ffn_gate[i], w_ffn_up[i],
                    w_ffn_down[i], norm_ple[i], w_ple[i], w_ple_gate[i], norm_final,
                    final=(i == depth - 1), tm=tm_tok)
    return x2d.reshape(B, S, D)
```

```python
import functools
import math

import jax
import jax.numpy as jnp
from jax import lax
from jax.experimental import pallas as pl
from jax.experimental.pallas import tpu as pltpu

F32 = jnp.float32
BF16 = jnp.bfloat16

N_ATTN_HEADS = 4
ATTN_HALF_DIM = 64
ATTN_V_DIM = 128
ATTN_WIDTH = 512
N_RWKV_HEADS = 8
RWKV_HEAD_DIM = 64
RWKV_WIDTH = 512
DECAY_LORA = 64
AAA_LORA = 64
GATE_LORA = 128
RWKV_COLS = 3 * RWKV_WIDTH + DECAY_LORA + AAA_LORA + GATE_LORA
NUM_BUCKETS = 32
MAX_DISTANCE = 128
NORM_EPS = 1e-6
SUBLN_EPS = 1e-5
GN_EPS = 64e-5

LANES = 128
BF16_SUBLANES = 16
VMEM_LIMIT_BYTES = 56 * 1024 * 1024
ATTN_TILE = 1024
ATTN_FAR_TILES = 2
ATTN_ROW_SPLIT = 4
BIAS_REACH = 128
LOG2_E = math.log2(math.e)
SCAN_CHUNK = 64
MASK_VALUE = -1e30


def _cparams(*sem):
    return pltpu.CompilerParams(dimension_semantics=sem, vmem_limit_bytes=VMEM_LIMIT_BYTES)


def _const_spec(shape):
    nd = len(shape)
    return pl.BlockSpec(shape, lambda *_: (0,) * nd, pipeline_mode=pl.Buffered(1))


def _dot(a, b):
    return jnp.dot(a.astype(BF16), b.astype(BF16), preferred_element_type=F32)


def _dot_nt(a, b):
    return lax.dot_general(a.astype(BF16), b.astype(BF16), (((1,), (1,)), ((), ())),
                           preferred_element_type=F32)


def _dot_tn(a, b):
    return lax.dot_general(a.astype(BF16), b.astype(BF16), (((0,), (0,)), ((), ())),
                           preferred_element_type=F32)


def _dot_hilo(a_exact_bf16, x):
    hi = x.astype(BF16)
    lo = (x - hi.astype(F32)).astype(BF16)
    return (jnp.dot(a_exact_bf16, hi, preferred_element_type=F32)
            + jnp.dot(a_exact_bf16, lo, preferred_element_type=F32))


def _rms(x, g, eps):
    return x * lax.rsqrt(jnp.mean(x * x, axis=-1, keepdims=True) + eps) * g


def _inproj_kernel(x_ref, g_ref, w_ref, qkv_ref, zr_ref, gates_ref):
    h = _rms(x_ref[...], g_ref[...], NORM_EPS).astype(BF16)
    scale = ATTN_HALF_DIM ** -0.5 * LOG2_E
    nq = 3 * ATTN_WIDTH
    for c0 in range(0, nq, 512):
        z = jnp.dot(h, w_ref[:, c0:c0 + 512], preferred_element_type=F32)
        if c0 < ATTN_WIDTH:
            z = z * scale
        qkv_ref[:, c0:c0 + 512] = z.astype(BF16)
    c0 = nq
    while c0 < nq + RWKV_COLS:
        w = min(512, nq + RWKV_COLS - c0)
        zr_ref[:, c0 - nq:c0 - nq + w] = jnp.dot(h, w_ref[:, c0:c0 + w],
                                                 preferred_element_type=F32).astype(zr_ref.dtype)
        c0 += w
    base = nq + RWKV_COLS
    for c0 in range(0, gates_ref.shape[1], 512):
        z = jnp.dot(h, w_ref[:, base + c0:base + c0 + 512], preferred_element_type=F32)
        gates_ref[:, c0:c0 + 512] = jax.nn.sigmoid(z).astype(BF16)


def _inproj(x2d, g, w_bf16, tm):
    T, D = x2d.shape
    ncols = w_bf16.shape[1]
    ngate = ncols - 3 * ATTN_WIDTH - RWKV_COLS
    return pl.pallas_call(
        _inproj_kernel,
        grid=(T // tm,),
        in_specs=[pl.BlockSpec((tm, D), lambda i: (i, 0)),
                  _const_spec((1, D)),
                  _const_spec((D, ncols))],
        out_specs=[pl.BlockSpec((tm, 3 * ATTN_WIDTH), lambda i: (i, 0)),
                   pl.BlockSpec((tm, RWKV_COLS), lambda i: (i, 0)),
                   pl.BlockSpec((tm, ngate), lambda i: (i, 0))],
        out_shape=[jax.ShapeDtypeStruct((T, 3 * ATTN_WIDTH), BF16),
                   jax.ShapeDtypeStruct((T, RWKV_COLS), BF16),
                   jax.ShapeDtypeStruct((T, ngate), BF16)],
        compiler_params=_cparams("parallel"),
        name="inproj",
    )(x2d, g.reshape(1, D), w_bf16)


def _bias_tiles_kernel(rb_ref, out_ref):
    hc = pl.program_id(0)
    tr, bw = out_ref.shape[-2:]
    rows = lax.broadcasted_iota(jnp.int32, (tr, bw), 0)
    cols = lax.broadcasted_iota(jnp.int32, (tr, bw), 1)
    max_exact = NUM_BUCKETS // 2
    n_hc = 2 * N_ATTN_HEADS
    far = rb_ref[(NUM_BUCKETS - 1) * n_hc + hc]
    dist = rows + BIAS_REACH - cols
    n = jnp.maximum(dist, 0)
    nf = jnp.maximum(n, max_exact).astype(F32)
    large = max_exact + (jnp.log(nf / max_exact) / math.log(MAX_DISTANCE / max_exact)
                         * (NUM_BUCKETS - max_exact)).astype(jnp.int32)
    large = jnp.minimum(large, NUM_BUCKETS - 1)
    bucket = jnp.where(n < max_exact, n, large)
    tile = jnp.zeros((tr, bw), F32)
    for b in range(NUM_BUCKETS):
        tile = jnp.where(bucket == b, rb_ref[b * n_hc + hc], tile)
    tile = (tile - far) * LOG2_E
    out_ref[0] = jnp.where(dist >= 0, tile, MASK_VALUE)


def _bias_tiles(rel_bias, tr):
    n_hc = 2 * N_ATTN_HEADS
    bw = BIAS_REACH + tr
    return pl.pallas_call(
        _bias_tiles_kernel,
        grid=(n_hc,),
        in_specs=[pl.BlockSpec(memory_space=pltpu.SMEM)],
        out_specs=pl.BlockSpec((1, tr, bw), lambda i: (i, 0, 0)),
        out_shape=jax.ShapeDtypeStruct((n_hc, tr, bw), F32),
        compiler_params=_cparams("parallel"),
        name="bias_tiles",
    )(rel_bias.reshape(-1).astype(F32))


def _attn_kernel(scal_ref, q_ref, k_ref, v_ref, bias_ref, g_ref, o_ref, m_ref, acc_ref, v1_ref, *,
                 far_tiles, row_split):
    t = q_ref.shape[1]
    tr = t // row_split
    qi = pl.program_id(2)

    @pl.when(qi == 0)
    def _():
        v1_ref[:, 0:ATTN_V_DIM] = v_ref[0]
        v1_ref[:, ATTN_V_DIM:] = jnp.ones((v1_ref.shape[0], LANES), BF16)

    lane = lax.broadcasted_iota(jnp.int32, (1, LANES), 1)
    q = q_ref[0]
    zero = jnp.zeros_like(q)
    qm = (jnp.where(lane < ATTN_HALF_DIM, q, zero), jnp.where(lane >= ATTN_HALF_DIM, q, zero))

    m_ref[...] = jnp.full(m_ref.shape, MASK_VALUE, F32)
    acc_ref[...] = jnp.zeros(acc_ref.shape, F32)

    streams = [(c, r) for c in range(2) for r in range(row_split)]

    def step(start, width, kind):
        kb = k_ref[0, pl.ds(start, width), :]
        vb = v1_ref[pl.ds(start, width), :]
        def scores(c, r):
            w = width if kind is None else width - t + (r + 1) * tr
            s = lax.dot_general(qm[c][r * tr:(r + 1) * tr], kb[0:w], (((1,), (1,)), ((), ())),
                                preferred_element_type=F32)
            if kind is not None:
                bw = min(bias_ref.shape[-1], w)
                band = bias_ref[0, c, :, bias_ref.shape[-1] - bw:]
                s = s + band if w == bw else jnp.concatenate([s[:, :w - bw], s[:, w - bw:] + band], axis=1)
            return s

        def softmax_pv(c, r, s):
            rows = slice(r * tr, (r + 1) * tr)
            w = s.shape[1]
            m_prev = m_ref[c, rows, :]
            m_new = jnp.maximum(m_prev, jnp.max(s, axis=-1, keepdims=True))
            alpha = jnp.exp2(m_prev - m_new)
            p = jnp.exp2(s - jnp.concatenate([m_new] * (w // LANES), axis=1))
            acc_ref[c, rows, :] = (jnp.concatenate([alpha, alpha], axis=1) * acc_ref[c, rows, :]
                                   + jnp.dot(p.astype(BF16), vb[0:w], preferred_element_type=F32))
            m_ref[c, rows, :] = m_new

        pending = [scores(c, r) for c, r in streams]
        for (c, r), s in zip(streams, pending):
            softmax_pv(c, r, s)

    n_far = jnp.maximum(qi - 1, 0)
    far_w = far_tiles * t

    def far_body(j, carry):
        step(pl.multiple_of(j * far_w, far_w), far_w, None)
        return carry

    n_wide = n_far // far_tiles
    lax.fori_loop(0, n_wide, far_body, 0)
    pos = n_wide * far_tiles
    rem = n_far - pos
    w_tiles = far_tiles // 2
    while w_tiles >= 1:
        take = rem >= w_tiles

        @pl.when(take)
        def _(pos=pos, w_tiles=w_tiles):
            step(pl.multiple_of(pos * t, t), w_tiles * t, None)

        pos = pos + jnp.where(take, w_tiles, 0)
        rem = rem - jnp.where(take, w_tiles, 0)
        w_tiles //= 2

    if k_ref.shape[1] >= 2 * t:
        @pl.when(qi >= 1)
        def _():
            step(pl.multiple_of((qi - 1) * t, t), 2 * t, "near")

    @pl.when(qi == 0)
    def _():
        step(0, t, "first")

    lam = scal_ref[0]
    out_scale = scal_ref[1]
    dv = ATTN_V_DIM
    o = acc_ref[0, :, 0:dv] / acc_ref[0, :, dv:] - lam * (acc_ref[1, :, 0:dv] / acc_ref[1, :, dv:])
    o_ref[0] = (_rms(o, g_ref[...], SUBLN_EPS) * out_scale).astype(o_ref.dtype)


def _attention(qkv, bias_tiles, scal, subln_g, t, far_tiles):
    B, S, _ = qkv.shape
    H = N_ATTN_HEADS
    return pl.pallas_call(
        functools.partial(_attn_kernel, far_tiles=far_tiles, row_split=ATTN_ROW_SPLIT),
        grid=(B, H, S // t),
        in_specs=[pl.BlockSpec(memory_space=pltpu.SMEM),
                  pl.BlockSpec((1, t, LANES), lambda b, h, i: (b, i, h)),
                  pl.BlockSpec((1, S, LANES), lambda b, h, i: (b, 0, H + h)),
                  pl.BlockSpec((1, S, LANES), lambda b, h, i: (b, 0, 2 * H + h)),
                  pl.BlockSpec((1, 2) + bias_tiles.shape[-2:], lambda b, h, i: (h, 0, 0, 0)),
                  _const_spec((1, ATTN_V_DIM))],
        out_specs=pl.BlockSpec((1, t, LANES), lambda b, h, i: (b, i, h)),
        out_shape=jax.ShapeDtypeStruct((B, S, ATTN_WIDTH), BF16),
        scratch_shapes=[pltpu.VMEM((2, t, LANES), F32),
                        pltpu.VMEM((2, t, ATTN_V_DIM + LANES), F32),
                        pltpu.VMEM((S, ATTN_V_DIM + LANES), BF16)],
        compiler_params=_cparams("parallel", "parallel", "arbitrary"),
        name="diff_attention",
    )(scal, qkv, qkv, qkv, bias_tiles.reshape((H, 2) + bias_tiles.shape[-2:]), subln_g.reshape(1, ATTN_V_DIM))


def _rwkv_prep_kernel(z_ref, zp_ref, mu_ref, w0_ref, w2_ref, a0_ref, a2_ref, g2_ref, kk_ref, ka_ref,
                      rk_ref, seg_ref,
                      r_ref, lw_ref, k_ref, v_ref, a_ref, b_ref, g_ref, bg_ref):
    i = pl.program_id(1)
    z = z_ref[0].astype(F32)
    tm = z.shape[0]
    W = RWKV_WIDTH
    last = zp_ref.shape[1] - 1
    prev_row = zp_ref[0, last:last + 1, :].astype(F32) * (i > 0).astype(F32)
    row = lax.broadcasted_iota(jnp.int32, (tm, 1), 0)
    prev = jnp.where(row == 0, prev_row, pltpu.roll(z, 1, 0))
    zs = z + (prev - z) * mu_ref[...]
    r = zs[:, 0:W]
    kr = zs[:, W:2 * W]
    vr = zs[:, 2 * W:3 * W]
    xwa = zs[:, 3 * W:3 * W + DECAY_LORA + AAA_LORA]
    xg = zs[:, 3 * W + DECAY_LORA + AAA_LORA:]

    dw = w0_ref[...] + _dot(jnp.tanh(xwa), w2_ref[...])
    softplus = jnp.maximum(-dw, 0.0) + jnp.log1p(jnp.exp(-jnp.abs(dw)))
    lw_ref[0] = -jnp.exp(-softplus - 0.5)
    asig = jax.nn.sigmoid(a0_ref[...] + _dot(xwa, a2_ref[...]))
    g = _dot(jax.nn.sigmoid(xg), g2_ref[...])

    seg = seg_ref[...]
    kk = kr * kk_ref[...]
    norm = jnp.sqrt(_dot(kk * kk, seg))
    kkn = kk / jnp.maximum(norm, 1e-12)
    kmod = kr * (1.0 + (asig - 1.0) * ka_ref[...])
    bonus = _dot(r * kmod * rk_ref[...], seg) * vr

    r_ref[0] = r.astype(r_ref.dtype)
    k_ref[0] = kmod.astype(k_ref.dtype)
    v_ref[0] = vr.astype(v_ref.dtype)
    a_ref[0] = (-kkn).astype(a_ref.dtype)
    b_ref[0] = (kkn * asig).astype(b_ref.dtype)
    g_ref[0] = g.astype(g_ref.dtype)
    bg_ref[0] = (bonus * g).astype(bg_ref.dtype)


def _rwkv_prep(zr, mu, w0, w2, a0, a2, g2, kkp, ka, rk, tm):
    B, S, C = zr.shape
    W = RWKV_WIDTH
    w2p = jnp.concatenate([w2, jnp.zeros_like(w2)], axis=0).astype(BF16)
    a2p = jnp.concatenate([jnp.zeros_like(a2), a2], axis=0).astype(BF16)
    head = jnp.arange(W) // RWKV_HEAD_DIM
    seg = (head[:, None] == head[None, :]).astype(BF16)
    row = lambda v: v.reshape(1, -1).astype(F32)
    sub = BF16_SUBLANES
    nblk = tm // sub
    out_spec = pl.BlockSpec((1, tm, W), lambda b, i: (b, i, 0))
    return pl.pallas_call(
        _rwkv_prep_kernel,
        grid=(B, S // tm),
        in_specs=[pl.BlockSpec((1, tm, C), lambda b, i: (b, i, 0)),
                  pl.BlockSpec((1, sub, C), lambda b, i: (b, jnp.maximum(i * nblk - 1, 0), 0)),
                  _const_spec((1, C)), _const_spec((1, W)), _const_spec((LANES, W)),
                  _const_spec((1, W)), _const_spec((LANES, W)), _const_spec((GATE_LORA, W)),
                  _const_spec((1, W)), _const_spec((1, W)), _const_spec((1, W)),
                  _const_spec((W, W))],
        out_specs=[out_spec] * 8,
        out_shape=[jax.ShapeDtypeStruct((B, S, W), F32 if n == 1 else BF16) for n in range(8)],
        compiler_params=_cparams("parallel", "parallel"),
        name="rwkv_prep",
    )(zr, zr, row(mu), row(w0), w2p, row(a0), a2p, g2.astype(BF16), row(kkp), row(ka), row(rk), seg)


def _scan_local(chunks, consts, tick):
    C = SCAN_CHUNK
    n = len(chunks)
    tril, strict2, incl2, lane_lo, lane_hi, bd = consts
    zc = jnp.zeros((C, LANES), F32)
    cat0 = lambda *xs: jnp.concatenate(xs, axis=0)
    cat1 = lambda *xs: jnp.concatenate(xs, axis=1)

    L_all = _dot_hilo(tril, cat1(*[ch[1] for ch in chunks]))
    pre = []
    for i, (r, lw, k, v, a, b) in enumerate(chunks):
        L = L_all[:, i * LANES:(i + 1) * LANES]
        winv = jnp.exp(-L)
        l_end = L[C - 1:C, :]
        wend = jnp.exp(l_end - L)
        rt = r * jnp.exp(L)
        at = a * jnp.exp(L - lw)
        bk = cat0(b * winv, k * winv)
        bk_end = cat0(b * wend, k * wend)
        pre.append((rt, at, bk, bk_end, jnp.exp(l_end)))
    tick()

    outs = []
    for rt, at, bk, _, _ in pre:
        ar = cat0(at, rt)
        zero = jnp.zeros_like(ar)
        outs.append(_dot_nt(cat0(jnp.where(lane_lo, ar, zero), jnp.where(lane_hi, ar, zero)), bk))
    tops = [(jnp.where(strict2, o[0:C], 0.0), jnp.where(strict2, o[2 * C:3 * C], 0.0)) for o in outs]
    bots = [cat0(jnp.where(incl2, o[C:2 * C], 0.0), jnp.where(incl2, o[3 * C:4 * C], 0.0)) for o in outs]
    tick()

    aak_v = [_dot(cat0(jnp.where(lane_hi, t0, zc), jnp.where(lane_hi, t1, zc)), cat0(ch[3], ch[3]))
             for (t0, t1), ch in zip(tops, chunks)]
    tick()
    A = [cat0(jnp.where(lane_lo, t0, zc), jnp.where(lane_hi, pltpu.roll(t1, C, 1), zc)) for t0, t1 in tops]
    Z = [cat1(cat0(p[1], p[1]), av) for p, av in zip(pre, aak_v)]
    n_steps = int(math.log2(C))
    for s in range(n_steps):
        Z = [z + _dot(a_, z) for a_, z in zip(A, Z)]
        tick()
        if s + 1 < n_steps:
            A = [_dot(a_, a_) for a_ in A]
            tick()
    ahat = [jnp.where(lane_lo, z[0:C, 0:LANES], z[C:2 * C, 0:LANES]) for z in Z]
    vhat = [jnp.where(lane_lo, z[0:C, LANES:], z[C:2 * C, LANES:]) for z in Z]

    yr2 = [_dot(bot, cat0(cat1(vh, ah), cat1(ch[3], zc)))
           for bot, vh, ah, ch in zip(bots, vhat, ahat, chunks)]
    tick()
    qn = [_dot_tn(cat0(cat1(ah, vh), cat1(zc, ch[3])), p[3])
          for ah, vh, ch, p in zip(ahat, vhat, chunks, pre)]
    tick()
    res = []
    for i in range(n):
        yhat = jnp.where(lane_lo, yr2[i][0:C, 0:LANES], yr2[i][C:2 * C, 0:LANES])
        rhat = pre[i][0] + jnp.where(lane_lo, yr2[i][0:C, LANES:], yr2[i][C:2 * C, LANES:])
        Q = jnp.where(bd, qn[i][0:LANES], 0.0)
        Nt = jnp.where(bd, qn[i][LANES:], 0.0)
        res.append((rhat, yhat, Q, Nt, pre[i][4]))
    return res


SCAN_LOCAL_STAGES = 5 + 2 * int(math.log2(SCAN_CHUNK)) - 1


def _rwkv_scan_kernel(r_ref, lw_ref, k_ref, v_ref, a_ref, b_ref, g_ref, bg_ref, lnw_ref, lnb_ref,
                      o_ref, s_ref, rh_ref, yh_ref, q_ref, n_ref, wc_ref):
    C = SCAN_CHUNK
    n_chunks = r_ref.shape[1] // C
    i = pl.program_id(2)
    n_blocks = pl.num_programs(2) - 1

    @pl.when(i == 0)
    def _():
        s_ref[...] = jnp.zeros(s_ref.shape, F32)
        rh_ref[...] = jnp.zeros(rh_ref.shape, rh_ref.dtype)
        yh_ref[...] = jnp.zeros(yh_ref.shape, F32)
        q_ref[...] = jnp.zeros(q_ref.shape, q_ref.dtype)
        n_ref[...] = jnp.zeros(n_ref.shape, F32)
        wc_ref[...] = jnp.zeros(wc_ref.shape, F32)

    ri = lax.broadcasted_iota(jnp.int32, (C, 2 * C), 0)
    ci = lax.broadcasted_iota(jnp.int32, (C, 2 * C), 1) % C
    strict2 = ci < ri
    incl2 = ci <= ri
    lane = lax.broadcasted_iota(jnp.int32, (1, LANES), 1)
    lane_lo = lane < RWKV_HEAD_DIM
    lane_hi = lane >= RWKV_HEAD_DIM
    r128 = lax.broadcasted_iota(jnp.int32, (LANES, LANES), 0)
    c128 = lax.broadcasted_iota(jnp.int32, (LANES, LANES), 1)
    bd = (r128 < RWKV_HEAD_DIM) == (c128 < RWKV_HEAD_DIM)
    tril = (lax.broadcasted_iota(jnp.int32, (C, C), 1) <= lax.broadcasted_iota(jnp.int32, (C, C), 0)).astype(BF16)
    consts = (tril, strict2, incl2, lane_lo, lane_hi, bd)
    inv_n = 1.0 / RWKV_HEAD_DIM

    def head_sums(x):
        lo = jnp.sum(jnp.where(lane_lo, x, 0.0), axis=-1, keepdims=True)
        hi = jnp.sum(jnp.where(lane_hi, x, 0.0), axis=-1, keepdims=True)
        return jnp.where(lane_lo, lo, hi)

    def recurrence_step(c, S):
        sl = slice(c * C, (c + 1) * C)
        y = _dot_nt(rh_ref[c], S) + yh_ref[c]
        S = S * wc_ref[c] + _dot(S, q_ref[c]) + n_ref[c]
        mean = head_sums(y) * inv_n
        d = y - mean
        var = head_sums(d * d) * inv_n
        yn = d * lax.rsqrt(var + GN_EPS) * lnw_ref[...] + lnb_ref[...]
        o_ref[0, sl, :] = (yn * g_ref[0, sl, :].astype(F32) + bg_ref[0, sl, :].astype(F32)).astype(o_ref.dtype)
        return S

    @pl.when(i < n_blocks)
    def _():
        state = [s_ref[...]]
        done = [0]
        ticks = [0]

        def tick():
            ticks[0] += 1
            target = min(n_chunks, (ticks[0] * n_chunks + SCAN_LOCAL_STAGES - 1) // SCAN_LOCAL_STAGES)
            while done[0] < target:
                state[0] = recurrence_step(done[0], state[0])
                done[0] += 1

        chunks = []
        for c in range(n_chunks):
            sl = slice(c * C, (c + 1) * C)
            chunks.append(tuple(ref[0, sl, :].astype(F32)
                                for ref in (r_ref, lw_ref, k_ref, v_ref, a_ref, b_ref)))
        local = _scan_local(chunks, consts, tick)
        assert done[0] == n_chunks
        s_ref[...] = state[0]
        for c in range(n_chunks):
            rhat, yhat, Q, Nt, wc = local[c]
            rh_ref[c] = rhat.astype(rh_ref.dtype)
            yh_ref[c] = yhat
            q_ref[c] = Q.astype(q_ref.dtype)
            n_ref[c] = Nt
            wc_ref[c] = wc

    @pl.when(i == n_blocks)
    def _():
        S = s_ref[...]
        for c in range(n_chunks):
            S = recurrence_step(c, S)


def _rwkv_scan(r, lw, k, v, a, b, g, bg, lnw, lnb, tm):
    B, S, W = r.shape
    n_pairs = W // LANES
    n_blocks = S // tm
    n_chunks = tm // SCAN_CHUNK
    cur = pl.BlockSpec((1, tm, LANES), lambda bb, p, i: (bb, jnp.minimum(i, n_blocks - 1), p))
    prev = pl.BlockSpec((1, tm, LANES), lambda bb, p, i: (bb, jnp.maximum(i - 1, 0), p))
    vec = pl.BlockSpec((1, LANES), lambda bb, p, i: (0, p))
    return pl.pallas_call(
        _rwkv_scan_kernel,
        grid=(B, n_pairs, n_blocks + 1),
        in_specs=[cur] * 6 + [prev, prev, vec, vec],
        out_specs=prev,
        out_shape=jax.ShapeDtypeStruct((B, S, W), BF16),
        scratch_shapes=[pltpu.VMEM((LANES, LANES), F32),
                        pltpu.VMEM((n_chunks, SCAN_CHUNK, LANES), BF16),
                        pltpu.VMEM((n_chunks, SCAN_CHUNK, LANES), F32),
                        pltpu.VMEM((n_chunks, LANES, LANES), BF16),
                        pltpu.VMEM((n_chunks, LANES, LANES), F32),
                        pltpu.VMEM((n_chunks, 1, LANES), F32)],
        compiler_params=_cparams("parallel", "parallel", "arbitrary"),
        name="rwkv_scan",
    )(r, lw, k, v, a, b, g, bg, lnw.reshape(1, W).astype(F32), lnb.reshape(1, W).astype(F32))


def _post_kernel(x_ref, oa_ref, yr_ref, gates_ref, p_ref, woa_ref, wor_ref, wout_ref, nffn_ref,
                 wg_ref, wu_ref, wd_ref, nple_ref, wple_ref, wpg_ref, nfin_ref, o_ref, *, final, ff_chunk):
    D = x_ref.shape[1]
    ya = jnp.dot(oa_ref[...], woa_ref[...], preferred_element_type=F32)
    yr = jnp.dot(yr_ref[...], wor_ref[...], preferred_element_type=F32)
    m = gates_ref[:, 0:D].astype(F32) * ya + gates_ref[:, D:2 * D].astype(F32) * yr
    x = x_ref[...] + _dot(m, wout_ref[...])

    h2 = _rms(x, nffn_ref[...], NORM_EPS).astype(BF16)
    d_ff = wg_ref.shape[1]
    acc = jnp.zeros_like(x)
    for c0 in range(0, d_ff, ff_chunk):
        gt = jnp.dot(h2, wg_ref[:, c0:c0 + ff_chunk], preferred_element_type=F32)
        up = jnp.dot(h2, wu_ref[:, c0:c0 + ff_chunk], preferred_element_type=F32)
        act = gt * jax.nn.sigmoid(gt) * up
        acc = acc + jnp.dot(act.astype(BF16), wd_ref[c0:c0 + ff_chunk, :], preferred_element_type=F32)
    x = x + acc

    e = _dot(p_ref[...], wple_ref[...])
    gp = jax.nn.sigmoid(_dot(_rms(x, nple_ref[...], NORM_EPS), wpg_ref[...]))
    x = x + gp * e
    if final:
        x = _rms(x, nfin_ref[...], NORM_EPS)
    o_ref[...] = x


def _post(x2d, oa, yr, gates, p3d, layer, woa, wor, wout, nffn, wg, wu, wd, nple, wple, wpg, nfin, final, tm):
    T, D = x2d.shape
    d_ff = wg.shape[1]
    ff_chunk = 256 if d_ff % 256 == 0 else LANES
    tok = lambda w: pl.BlockSpec((tm, w), lambda i: (i, 0))
    p_spec = pl.BlockSpec((None, tm, p3d.shape[2]), lambda i: (layer, i, 0))
    row = lambda v: v.reshape(1, -1).astype(F32)
    bf = lambda w: w.astype(BF16)
    return pl.pallas_call(
        functools.partial(_post_kernel, final=final, ff_chunk=ff_chunk),
        grid=(T // tm,),
        in_specs=[tok(D), tok(oa.shape[1]), tok(yr.shape[1]), tok(gates.shape[1]), p_spec,
                  _const_spec(woa.shape), _const_spec(wor.shape), _const_spec(wout.shape),
                  _const_spec((1, D)), _const_spec(wg.shape), _const_spec(wu.shape), _const_spec(wd.shape),
                  _const_spec((1, D)), _const_spec(wple.shape), _const_spec(wpg.shape), _const_spec((1, D))],
        out_specs=tok(D),
        out_shape=jax.ShapeDtypeStruct((T, D), F32),
        compiler_params=_cparams("parallel"),
        name="post",
    )(x2d, oa, yr, gates, p3d, bf(woa), bf(wor), bf(wout), row(nffn), bf(wg), bf(wu), bf(wd),
      row(nple), bf(wple), bf(wpg), row(nfin))


def _tile(n, pref):
    t = min(n, pref)
    assert n % t == 0, (n, t)
    return t


def kernel(x, p, rel_bias, norm_mix, w_in, lam_q1, lam_k1, lam_q2, lam_k2, attn_subln, rwkv_mu, rwkv_w0, rwkv_w2, rwkv_a0, rwkv_a2, rwkv_g2, rwkv_kk, rwkv_ka, rwkv_rk, rwkv_lnx_w, rwkv_lnx_b, w_out_attn, w_out_rwkv, w_out, norm_ffn, w_ffn_gate, w_ffn_up, w_ffn_down, norm_ple, w_ple, w_ple_gate, norm_final):
    B, S, D = x.shape
    depth = w_in.shape[0]
    T = B * S
    t_attn = _tile(S, ATTN_TILE)
    tr_attn = t_attn // ATTN_ROW_SPLIT
    assert tr_attn % LANES == 0
    tm_tok = _tile(T, 512)
    tm_prep = _tile(S, 512)
    tm_scan = _tile(S, 16 * SCAN_CHUNK)
    assert tm_scan % SCAN_CHUNK == 0

    bias_tiles = _bias_tiles(rel_bias, tr_attn)
    x2d = x.reshape(T, D)
    for i in range(depth):
        qkv, zr, gates = _inproj(x2d, norm_mix[i], w_in[i].astype(BF16), tm_tok)

        lam_init = 0.8 - 0.6 * math.exp(-0.3 * i)
        lam = (jnp.exp(jnp.sum(lam_q1[i] * lam_k1[i])) - jnp.exp(jnp.sum(lam_q2[i] * lam_k2[i])) + lam_init)
        scal = jnp.stack([lam.astype(F32), jnp.asarray(1.0 - lam_init, F32)])
        oa = _attention(qkv.reshape(B, S, -1), bias_tiles, scal, attn_subln[i], t_attn,
                        max(1, min(ATTN_FAR_TILES, S // t_attn)))

        r, lw, k, v, a, b, g, bg = _rwkv_prep(zr.reshape(B, S, -1), rwkv_mu[i], rwkv_w0[i], rwkv_w2[i],
                                              rwkv_a0[i], rwkv_a2[i], rwkv_g2[i], rwkv_kk[i], rwkv_ka[i],
                                              rwkv_rk[i].reshape(-1), tm_prep)
        yr = _rwkv_scan(r, lw, k, v, a, b, g, bg, rwkv_lnx_w[i], rwkv_lnx_b[i], tm_scan)

        x2d = _post(x2d, oa.reshape(T, -1), yr.reshape(T, -1), gates, p.reshape(depth, T, -1), i,
                    w_out_attn[i], w_out_rwkv[i], w_out[i], norm_ffn[i], w_ffn_gate[i], w_ffn_up[i],
                    w_ffn_down[i], norm_ple[i], w_ple[i], w_ple_gate[i], norm_final,
                    final=(i == depth - 1), tm=tm_tok)
    return x2d.reshape(B, S, D)
```

```python
import functools
import math

import jax
import jax.numpy as jnp
from jax import lax
from jax.experimental import pallas as pl
from jax.experimental.pallas import tpu as pltpu

F32 = jnp.float32
BF16 = jnp.bfloat16

N_ATTN_HEADS = 4
ATTN_HALF_DIM = 64
ATTN_V_DIM = 128
ATTN_WIDTH = 512
N_RWKV_HEADS = 8
RWKV_HEAD_DIM = 64
RWKV_WIDTH = 512
DECAY_LORA = 64
AAA_LORA = 64
GATE_LORA = 128
RWKV_COLS = 3 * RWKV_WIDTH + DECAY_LORA + AAA_LORA + GATE_LORA
NUM_BUCKETS = 32
MAX_DISTANCE = 128
NORM_EPS = 1e-6
SUBLN_EPS = 1e-5
GN_EPS = 64e-5

LANES = 128
BF16_SUBLANES = 16
VMEM_LIMIT_BYTES = 56 * 1024 * 1024
ATTN_TILE = 1024
ATTN_FAR_TILES = 2
ATTN_ROW_SPLIT = 4
BIAS_REACH = 128
LOG2_E = math.log2(math.e)
SCAN_CHUNK = 64
MASK_VALUE = -1e30


def _cparams(*sem):
    return pltpu.CompilerParams(dimension_semantics=sem, vmem_limit_bytes=VMEM_LIMIT_BYTES)


def _const_spec(shape):
    nd = len(shape)
    return pl.BlockSpec(shape, lambda *_: (0,) * nd, pipeline_mode=pl.Buffered(1))


def _dot(a, b):
    return jnp.dot(a.astype(BF16), b.astype(BF16), preferred_element_type=F32)


def _dot_nt(a, b):
    return lax.dot_general(a.astype(BF16), b.astype(BF16), (((1,), (1,)), ((), ())),
                           preferred_element_type=F32)


def _dot_tn(a, b):
    return lax.dot_general(a.astype(BF16), b.astype(BF16), (((0,), (0,)), ((), ())),
                           preferred_element_type=F32)


def _dot_hilo(a_exact_bf16, x):
    hi = x.astype(BF16)
    lo = (x - hi.astype(F32)).astype(BF16)
    return (jnp.dot(a_exact_bf16, hi, preferred_element_type=F32)
            + jnp.dot(a_exact_bf16, lo, preferred_element_type=F32))


def _rms(x, g, eps):
    return x * lax.rsqrt(jnp.mean(x * x, axis=-1, keepdims=True) + eps) * g


def _inproj_kernel(x_ref, g_ref, w_ref, qkv_ref, zr_ref, gates_ref):
    h = _rms(x_ref[...], g_ref[...], NORM_EPS).astype(BF16)
    scale = ATTN_HALF_DIM ** -0.5 * LOG2_E
    nq = 3 * ATTN_WIDTH
    for c0 in range(0, nq, 512):
        z = jnp.dot(h, w_ref[:, c0:c0 + 512], preferred_element_type=F32)
        if c0 < ATTN_WIDTH:
            z = z * scale
        qkv_ref[:, c0:c0 + 512] = z.astype(BF16)
    c0 = nq
    while c0 < nq + RWKV_COLS:
        w = min(512, nq + RWKV_COLS - c0)
        zr_ref[:, c0 - nq:c0 - nq + w] = jnp.dot(h, w_ref[:, c0:c0 + w],
                                                 preferred_element_type=F32).astype(zr_ref.dtype)
        c0 += w
    base = nq + RWKV_COLS
    for c0 in range(0, gates_ref.shape[1], 512):
        z = jnp.dot(h, w_ref[:, base + c0:base + c0 + 512], preferred_element_type=F32)
        gates_ref[:, c0:c0 + 512] = jax.nn.sigmoid(z).astype(BF16)


def _inproj(x2d, g, w_bf16, tm):
    T, D = x2d.shape
    ncols = w_bf16.shape[1]
    ngate = ncols - 3 * ATTN_WIDTH - RWKV_COLS
    return pl.pallas_call(
        _inproj_kernel,
        grid=(T // tm,),
        in_specs=[pl.BlockSpec((tm, D), lambda i: (i, 0)),
                  _const_spec((1, D)),
                  _const_spec((D, ncols))],
        out_specs=[pl.BlockSpec((tm, 3 * ATTN_WIDTH), lambda i: (i, 0)),
                   pl.BlockSpec((tm, RWKV_COLS), lambda i: (i, 0)),
                   pl.BlockSpec((tm, ngate), lambda i: (i, 0))],
        out_shape=[jax.ShapeDtypeStruct((T, 3 * ATTN_WIDTH), BF16),
                   jax.ShapeDtypeStruct((T, RWKV_COLS), BF16),
                   jax.ShapeDtypeStruct((T, ngate), BF16)],
        compiler_params=_cparams("parallel"),
        name="inproj",
    )(x2d, g.reshape(1, D), w_bf16)


def _bias_tiles_kernel(rb_ref, out_ref):
    hc = pl.program_id(0)
    tr, bw = out_ref.shape[-2:]
    rows = lax.broadcasted_iota(jnp.int32, (tr, bw), 0)
    cols = lax.broadcasted_iota(jnp.int32, (tr, bw), 1)
    max_exact = NUM_BUCKETS // 2
    n_hc = 2 * N_ATTN_HEADS
    far = rb_ref[(NUM_BUCKETS - 1) * n_hc + hc]
    dist = rows + BIAS_REACH - cols
    n = jnp.maximum(dist, 0)
    nf = jnp.maximum(n, max_exact).astype(F32)
    large = max_exact + (jnp.log(nf / max_exact) / math.log(MAX_DISTANCE / max_exact)
                         * (NUM_BUCKETS - max_exact)).astype(jnp.int32)
    large = jnp.minimum(large, NUM_BUCKETS - 1)
    bucket = jnp.where(n < max_exact, n, large)
    tile = jnp.zeros((tr, bw), F32)
    for b in range(NUM_BUCKETS):
        tile = jnp.where(bucket == b, rb_ref[b * n_hc + hc], tile)
    tile = (tile - far) * LOG2_E
    out_ref[0] = jnp.where(dist >= 0, tile, MASK_VALUE)


def _bias_tiles(rel_bias, tr):
    n_hc = 2 * N_ATTN_HEADS
    bw = BIAS_REACH + tr
    return pl.pallas_call(
        _bias_tiles_kernel,
        grid=(n_hc,),
        in_specs=[pl.BlockSpec(memory_space=pltpu.SMEM)],
        out_specs=pl.BlockSpec((1, tr, bw), lambda i: (i, 0, 0)),
        out_shape=jax.ShapeDtypeStruct((n_hc, tr, bw), F32),
        compiler_params=_cparams("parallel"),
        name="bias_tiles",
    )(rel_bias.reshape(-1).astype(F32))


def _attn_kernel(scal_ref, q_ref, k_ref, v_ref, bias_ref, g_ref, o_ref, m_ref, acc_ref, v1_ref, *,
                 far_tiles, row_split):
    t = q_ref.shape[1]
    tr = t // row_split
    qi = pl.program_id(2)

    @pl.when(qi == 0)
    def _():
        v1_ref[:, 0:ATTN_V_DIM] = v_ref[0]
        v1_ref[:, ATTN_V_DIM:] = jnp.ones((v1_ref.shape[0], LANES), BF16)

    lane = lax.broadcasted_iota(jnp.int32, (1, LANES), 1)
    q = q_ref[0]
    zero = jnp.zeros_like(q)
    qm = (jnp.where(lane < ATTN_HALF_DIM, q, zero), jnp.where(lane >= ATTN_HALF_DIM, q, zero))

    m_ref[...] = jnp.full(m_ref.shape, MASK_VALUE, F32)
    acc_ref[...] = jnp.zeros(acc_ref.shape, F32)

    streams = [(c, r) for c in range(2) for r in range(row_split)]

    def step(start, width, kind):
        kb = k_ref[0, pl.ds(start, width), :]
        vb = v1_ref[pl.ds(start, width), :]
        def scores(c, r):
            w = width if kind is None else width - t + (r + 1) * tr
            s = lax.dot_general(qm[c][r * tr:(r + 1) * tr], kb[0:w], (((1,), (1,)), ((), ())),
                                preferred_element_type=F32)
            if kind is not None:
                bw = min(bias_ref.shape[-1], w)
                band = bias_ref[0, c, :, bias_ref.shape[-1] - bw:]
                s = s + band if w == bw else jnp.concatenate([s[:, :w - bw], s[:, w - bw:] + band], axis=1)
            return s

        def softmax_pv(c, r, s):
            rows = slice(r * tr, (r + 1) * tr)
            w = s.shape[1]
            m_prev = m_ref[c, rows, :]
            m_new = jnp.maximum(m_prev, jnp.max(s, axis=-1, keepdims=True))
            alpha = jnp.exp2(m_prev - m_new)
            p = jnp.exp2(s - jnp.concatenate([m_new] * (w // LANES), axis=1))
            acc_ref[c, rows, :] = (jnp.concatenate([alpha, alpha], axis=1) * acc_ref[c, rows, :]
                                   + jnp.dot(p.astype(BF16), vb[0:w], preferred_element_type=F32))
            m_ref[c, rows, :] = m_new

        pending = [scores(c, r) for c, r in streams]
        for (c, r), s in zip(streams, pending):
            softmax_pv(c, r, s)

    n_far = jnp.maximum(qi - 1, 0)
    far_w = far_tiles * t

    def far_body(j, carry):
        step(pl.multiple_of(j * far_w, far_w), far_w, None)
        return carry

    n_wide = n_far // far_tiles
    lax.fori_loop(0, n_wide, far_body, 0)
    pos = n_wide * far_tiles
    rem = n_far - pos
    w_tiles = far_tiles // 2
    while w_tiles >= 1:
        take = rem >= w_tiles

        @pl.when(take)
        def _(pos=pos, w_tiles=w_tiles):
            step(pl.multiple_of(pos * t, t), w_tiles * t, None)

        pos = pos + jnp.where(take, w_tiles, 0)
        rem = rem - jnp.where(take, w_tiles, 0)
        w_tiles //= 2

    if k_ref.shape[1] >= 2 * t:
        @pl.when(qi >= 1)
        def _():
            step(pl.multiple_of((qi - 1) * t, t), 2 * t, "near")

    @pl.when(qi == 0)
    def _():
        step(0, t, "first")

    lam = scal_ref[0]
    out_scale = scal_ref[1]
    dv = ATTN_V_DIM
    o = acc_ref[0, :, 0:dv] / acc_ref[0, :, dv:] - lam * (acc_ref[1, :, 0:dv] / acc_ref[1, :, dv:])
    o_ref[0] = (_rms(o, g_ref[...], SUBLN_EPS) * out_scale).astype(o_ref.dtype)


def _attention(qkv, bias_tiles, scal, subln_g, t, far_tiles):
    B, S, _ = qkv.shape
    H = N_ATTN_HEADS
    return pl.pallas_call(
        functools.partial(_attn_kernel, far_tiles=far_tiles, row_split=ATTN_ROW_SPLIT),
        grid=(B, H, S // t),
        in_specs=[pl.BlockSpec(memory_space=pltpu.SMEM),
                  pl.BlockSpec((1, t, LANES), lambda b, h, i: (b, i, h)),
                  pl.BlockSpec((1, S, LANES), lambda b, h, i: (b, 0, H + h)),
                  pl.BlockSpec((1, S, LANES), lambda b, h, i: (b, 0, 2 * H + h)),
                  pl.BlockSpec((1, 2) + bias_tiles.shape[-2:], lambda b, h, i: (h, 0, 0, 0)),
                  _const_spec((1, ATTN_V_DIM))],
        out_specs=pl.BlockSpec((1, t, LANES), lambda b, h, i: (b, i, h)),
        out_shape=jax.ShapeDtypeStruct((B, S, ATTN_WIDTH), BF16),
        scratch_shapes=[pltpu.VMEM((2, t, LANES), F32),
                        pltpu.VMEM((2, t, ATTN_V_DIM + LANES), F32),
                        pltpu.VMEM((S, ATTN_V_DIM + LANES), BF16)],
        compiler_params=_cparams("parallel", "parallel", "arbitrary"),
        name="diff_attention",
    )(scal, qkv, qkv, qkv, bias_tiles.reshape((H, 2) + bias_tiles.shape[-2:]), subln_g.reshape(1, ATTN_V_DIM))


def _rwkv_prep_kernel(z_ref, zp_ref, mu_ref, w0_ref, w2_ref, a0_ref, a2_ref, g2_ref, kk_ref, ka_ref,
                      rk_ref, seg_ref,
                      r_ref, lw_ref, k_ref, v_ref, a_ref, b_ref, g_ref, bg_ref):
    i = pl.program_id(1)
    z = z_ref[0].astype(F32)
    tm = z.shape[0]
    W = RWKV_WIDTH
    last = zp_ref.shape[1] - 1
    prev_row = zp_ref[0, last:last + 1, :].astype(F32) * (i > 0).astype(F32)
    row = lax.broadcasted_iota(jnp.int32, (tm, 1), 0)
    prev = jnp.where(row == 0, prev_row, pltpu.roll(z, 1, 0))
    zs = z + (prev - z) * mu_ref[...]
    r = zs[:, 0:W]
    kr = zs[:, W:2 * W]
    vr = zs[:, 2 * W:3 * W]
    xwa = zs[:, 3 * W:3 * W + DECAY_LORA + AAA_LORA]
    xg = zs[:, 3 * W + DECAY_LORA + AAA_LORA:]

    dw = w0_ref[...] + _dot(jnp.tanh(xwa), w2_ref[...])
    softplus = jnp.maximum(-dw, 0.0) + jnp.log1p(jnp.exp(-jnp.abs(dw)))
    lw_ref[0] = -jnp.exp(-softplus - 0.5)
    asig = jax.nn.sigmoid(a0_ref[...] + _dot(xwa, a2_ref[...]))
    g = _dot(jax.nn.sigmoid(xg), g2_ref[...])

    seg = seg_ref[...]
    kk = kr * kk_ref[...]
    norm = jnp.sqrt(_dot(kk * kk, seg))
    kkn = kk / jnp.maximum(norm, 1e-12)
    kmod = kr * (1.0 + (asig - 1.0) * ka_ref[...])
    bonus = _dot(r * kmod * rk_ref[...], seg) * vr

    r_ref[0] = r.astype(r_ref.dtype)
    k_ref[0] = kmod.astype(k_ref.dtype)
    v_ref[0] = vr.astype(v_ref.dtype)
    a_ref[0] = (-kkn).astype(a_ref.dtype)
    b_ref[0] = (kkn * asig).astype(b_ref.dtype)
    g_ref[0] = g.astype(g_ref.dtype)
    bg_ref[0] = (bonus * g).astype(bg_ref.dtype)


def _rwkv_prep(zr, mu, w0, w2, a0, a2, g2, kkp, ka, rk, tm):
    B, S, C = zr.shape
    W = RWKV_WIDTH
    w2p = jnp.concatenate([w2, jnp.zeros_like(w2)], axis=0).astype(BF16)
    a2p = jnp.concatenate([jnp.zeros_like(a2), a2], axis=0).astype(BF16)
    head = jnp.arange(W) // RWKV_HEAD_DIM
    seg = (head[:, None] == head[None, :]).astype(BF16)
    row = lambda v: v.reshape(1, -1).astype(F32)
    sub = BF16_SUBLANES
    nblk = tm // sub
    out_spec = pl.BlockSpec((1, tm, W), lambda b, i: (b, i, 0))
    return pl.pallas_call(
        _rwkv_prep_kernel,
        grid=(B, S // tm),
        in_specs=[pl.BlockSpec((1, tm, C), lambda b, i: (b, i, 0)),
                  pl.BlockSpec((1, sub, C), lambda b, i: (b, jnp.maximum(i * nblk - 1, 0), 0)),
                  _const_spec((1, C)), _const_spec((1, W)), _const_spec((LANES, W)),
                  _const_spec((1, W)), _const_spec((LANES, W)), _const_spec((GATE_LORA, W)),
                  _const_spec((1, W)), _const_spec((1, W)), _const_spec((1, W)),
                  _const_spec((W, W))],
        out_specs=[out_spec] * 8,
        out_shape=[jax.ShapeDtypeStruct((B, S, W), F32 if n == 1 else BF16) for n in range(8)],
        compiler_params=_cparams("parallel", "parallel"),
        name="rwkv_prep",
    )(zr, zr, row(mu), row(w0), w2p, row(a0), a2p, g2.astype(BF16), row(kkp), row(ka), row(rk), seg)


def _scan_local(chunks, consts, tick):
    C = SCAN_CHUNK
    n = len(chunks)
    tril, strict4, incl4, bd = consts
    zc = jnp.zeros((C, LANES), F32)
    cat0 = lambda *xs: jnp.concatenate(xs, axis=0)
    cat1 = lambda *xs: jnp.concatenate(xs, axis=1)

    def by_head(x):
        lo = lax.broadcasted_iota(jnp.int32, (1, x.shape[1]), 1) % LANES < RWKV_HEAD_DIM
        return cat0(jnp.where(lo, x, 0.0), jnp.where(lo, 0.0, x))

    L_all = _dot_hilo(tril, cat1(*[ch[1] for ch in chunks]))
    pre = []
    for i, (r, lw, k, v, a, b) in enumerate(chunks):
        L = L_all[:, i * LANES:(i + 1) * LANES]
        winv = jnp.exp(-L)
        l_end = L[C - 1:C, :]
        wend = jnp.exp(l_end - L)
        rt = r * jnp.exp(L)
        at = a * jnp.exp(L - lw)
        bk4 = cat0(by_head(b * winv), by_head(k * winv))
        bk_end = cat0(b * wend, k * wend)
        pre.append((rt, at, bk4, bk_end, jnp.exp(l_end)))
    tick()

    outs = [_dot_nt(cat0(at, rt), bk4) for rt, at, bk4, _, _ in pre]
    tops = [jnp.where(strict4, o[0:C], 0.0) for o in outs]
    bots = [jnp.where(incl4, o[C:2 * C], 0.0) for o in outs]
    tick()

    aak_v = [_dot(top[:, 2 * C:], by_head(ch[3])) for top, ch in zip(tops, chunks)]
    tick()
    A = [top[:, :2 * C] for top in tops]
    Z = [cat1(p[1], av) for p, av in zip(pre, aak_v)]
    n_steps = int(math.log2(C))
    for s in range(n_steps):
        Z = [z + _dot(a_, by_head(z)) for a_, z in zip(A, Z)]
        tick()
        if s + 1 < n_steps:
            A = [_dot(a_, by_head(a_)) for a_ in A]
            tick()
    ahat = [z[:, :LANES] for z in Z]
    vhat = [z[:, LANES:] for z in Z]

    zc2 = jnp.zeros((2 * C, LANES), F32)
    yr2 = [_dot(bot, cat0(by_head(cat1(vh, ah)), cat1(by_head(ch[3]), zc2)))
           for bot, vh, ah, ch in zip(bots, vhat, ahat, chunks)]
    tick()
    qn = [_dot_tn(cat0(cat1(ah, vh), cat1(zc, ch[3])), p[3])
          for ah, vh, ch, p in zip(ahat, vhat, chunks, pre)]
    tick()
    res = []
    for i in range(n):
        yhat = yr2[i][:, :LANES]
        rhat = pre[i][0] + yr2[i][:, LANES:]
        Q = jnp.where(bd, qn[i][0:LANES], 0.0)
        Nt = jnp.where(bd, qn[i][LANES:], 0.0)
        res.append((rhat, yhat, Q, Nt, pre[i][4]))
    return res


SCAN_LOCAL_STAGES = 5 + 2 * int(math.log2(SCAN_CHUNK)) - 1


def _rwkv_scan_kernel(r_ref, lw_ref, k_ref, v_ref, a_ref, b_ref, g_ref, bg_ref, lnw_ref, lnb_ref,
                      o_ref, s_ref, rh_ref, yh_ref, q_ref, n_ref, wc_ref):
    C = SCAN_CHUNK
    n_chunks = r_ref.shape[1] // C
    i = pl.program_id(2)
    n_blocks = pl.num_programs(2) - 1

    @pl.when(i == 0)
    def _():
        s_ref[...] = jnp.zeros(s_ref.shape, F32)
        rh_ref[...] = jnp.zeros(rh_ref.shape, rh_ref.dtype)
        yh_ref[...] = jnp.zeros(yh_ref.shape, F32)
        q_ref[...] = jnp.zeros(q_ref.shape, q_ref.dtype)
        n_ref[...] = jnp.zeros(n_ref.shape, F32)
        wc_ref[...] = jnp.zeros(wc_ref.shape, F32)

    ri = lax.broadcasted_iota(jnp.int32, (C, 4 * C), 0)
    ci = lax.broadcasted_iota(jnp.int32, (C, 4 * C), 1) % C
    strict4 = ci < ri
    incl4 = ci <= ri
    lane = lax.broadcasted_iota(jnp.int32, (1, LANES), 1)
    lane_lo = lane < RWKV_HEAD_DIM
    lane_hi = lane >= RWKV_HEAD_DIM
    r128 = lax.broadcasted_iota(jnp.int32, (LANES, LANES), 0)
    c128 = lax.broadcasted_iota(jnp.int32, (LANES, LANES), 1)
    bd = (r128 < RWKV_HEAD_DIM) == (c128 < RWKV_HEAD_DIM)
    tril = (lax.broadcasted_iota(jnp.int32, (C, C), 1) <= lax.broadcasted_iota(jnp.int32, (C, C), 0)).astype(BF16)
    consts = (tril, strict4, incl4, bd)
    inv_n = 1.0 / RWKV_HEAD_DIM

    def head_sums(x):
        lo = jnp.sum(jnp.where(lane_lo, x, 0.0), axis=-1, keepdims=True)
        hi = jnp.sum(jnp.where(lane_hi, x, 0.0), axis=-1, keepdims=True)
        return jnp.where(lane_lo, lo, hi)

    def recurrence_step(c, S):
        sl = slice(c * C, (c + 1) * C)
        y = _dot_nt(rh_ref[c], S) + yh_ref[c]
        S = S * wc_ref[c] + _dot(S, q_ref[c]) + n_ref[c]
        mean = head_sums(y) * inv_n
        d = y - mean
        var = head_sums(d * d) * inv_n
        yn = d * lax.rsqrt(var + GN_EPS) * lnw_ref[...] + lnb_ref[...]
        o_ref[0, sl, :] = (yn * g_ref[0, sl, :].astype(F32) + bg_ref[0, sl, :].astype(F32)).astype(o_ref.dtype)
        return S

    @pl.when(i < n_blocks)
    def _():
        state = [s_ref[...]]
        done = [0]
        ticks = [0]

        def tick():
            ticks[0] += 1
            target = min(n_chunks, (ticks[0] * n_chunks + SCAN_LOCAL_STAGES - 1) // SCAN_LOCAL_STAGES)
            while done[0] < target:
                state[0] = recurrence_step(done[0], state[0])
                done[0] += 1

        chunks = []
        for c in range(n_chunks):
            sl = slice(c * C, (c + 1) * C)
            chunks.append(tuple(ref[0, sl, :].astype(F32)
                                for ref in (r_ref, lw_ref, k_ref, v_ref, a_ref, b_ref)))
        local = _scan_local(chunks, consts, tick)
        assert done[0] == n_chunks
        s_ref[...] = state[0]
        for c in range(n_chunks):
            rhat, yhat, Q, Nt, wc = local[c]
            rh_ref[c] = rhat.astype(rh_ref.dtype)
            yh_ref[c] = yhat
            q_ref[c] = Q.astype(q_ref.dtype)
            n_ref[c] = Nt
            wc_ref[c] = wc

    @pl.when(i == n_blocks)
    def _():
        S = s_ref[...]
        for c in range(n_chunks):
            S = recurrence_step(c, S)


def _rwkv_scan(r, lw, k, v, a, b, g, bg, lnw, lnb, tm):
    B, S, W = r.shape
    n_pairs = W // LANES
    n_blocks = S // tm
    n_chunks = tm // SCAN_CHUNK
    cur = pl.BlockSpec((1, tm, LANES), lambda bb, p, i: (bb, jnp.minimum(i, n_blocks - 1), p))
    prev = pl.BlockSpec((1, tm, LANES), lambda bb, p, i: (bb, jnp.maximum(i - 1, 0), p))
    vec = pl.BlockSpec((1, LANES), lambda bb, p, i: (0, p))
    return pl.pallas_call(
        _rwkv_scan_kernel,
        grid=(B, n_pairs, n_blocks + 1),
        in_specs=[cur] * 6 + [prev, prev, vec, vec],
        out_specs=prev,
        out_shape=jax.ShapeDtypeStruct((B, S, W), BF16),
        scratch_shapes=[pltpu.VMEM((LANES, LANES), F32),
                        pltpu.VMEM((n_chunks, SCAN_CHUNK, LANES), BF16),
                        pltpu.VMEM((n_chunks, SCAN_CHUNK, LANES), F32),
                        pltpu.VMEM((n_chunks, LANES, LANES), BF16),
                        pltpu.VMEM((n_chunks, LANES, LANES), F32),
                        pltpu.VMEM((n_chunks, 1, LANES), F32)],
        compiler_params=_cparams("parallel", "parallel", "arbitrary"),
        name="rwkv_scan",
    )(r, lw, k, v, a, b, g, bg, lnw.reshape(1, W).astype(F32), lnb.reshape(1, W).astype(F32))


def _post_kernel(x_ref, oa_ref, yr_ref, gates_ref, p_ref, woa_ref, wor_ref, wout_ref, nffn_ref,
                 wg_ref, wu_ref, wd_ref, nple_ref, wple_ref, wpg_ref, nfin_ref, o_ref, *, final, ff_chunk):
    D = x_ref.shape[1]
    ya = jnp.dot(oa_ref[...], woa_ref[...], preferred_element_type=F32)
    yr = jnp.dot(yr_ref[...], wor_ref[...], preferred_element_type=F32)
    m = gates_ref[:, 0:D].astype(F32) * ya + gates_ref[:, D:2 * D].astype(F32) * yr
    x = x_ref[...] + _dot(m, wout_ref[...])

    h2 = _rms(x, nffn_ref[...], NORM_EPS).astype(BF16)
    d_ff = wg_ref.shape[1]
    acc = jnp.zeros_like(x)
    for c0 in range(0, d_ff, ff_chunk):
        gt = jnp.dot(h2, wg_ref[:, c0:c0 + ff_chunk], preferred_element_type=F32)
        up = jnp.dot(h2, wu_ref[:, c0:c0 + ff_chunk], preferred_element_type=F32)
        act = gt * jax.nn.sigmoid(gt) * up
        acc = acc + jnp.dot(act.astype(BF16), wd_ref[c0:c0 + ff_chunk, :], preferred_element_type=F32)
    x = x + acc

    e = _dot(p_ref[...], wple_ref[...])
    gp = jax.nn.sigmoid(_dot(_rms(x, nple_ref[...], NORM_EPS), wpg_ref[...]))
    x = x + gp * e
    if final:
        x = _rms(x, nfin_ref[...], NORM_EPS)
    o_ref[...] = x


def _post(x2d, oa, yr, gates, p3d, layer, woa, wor, wout, nffn, wg, wu, wd, nple, wple, wpg, nfin, final, tm):
    T, D = x2d.shape
    d_ff = wg.shape[1]
    ff_chunk = 256 if d_ff % 256 == 0 else LANES
    tok = lambda w: pl.BlockSpec((tm, w), lambda i: (i, 0))
    p_spec = pl.BlockSpec((None, tm, p3d.shape[2]), lambda i: (layer, i, 0))
    row = lambda v: v.reshape(1, -1).astype(F32)
    bf = lambda w: w.astype(BF16)
    return pl.pallas_call(
        functools.partial(_post_kernel, final=final, ff_chunk=ff_chunk),
        grid=(T // tm,),
        in_specs=[tok(D), tok(oa.shape[1]), tok(yr.shape[1]), tok(gates.shape[1]), p_spec,
                  _const_spec(woa.shape), _const_spec(wor.shape), _const_spec(wout.shape),
                  _const_spec((1, D)), _const_spec(wg.shape), _const_spec(wu.shape), _const_spec(wd.shape),
                  _const_spec((1, D)), _const_spec(wple.shape), _const_spec(wpg.shape), _const_spec((1, D))],
        out_specs=tok(D),
        out_shape=jax.ShapeDtypeStruct((T, D), F32),
        compiler_params=_cparams("parallel"),
        name="post",
    )(x2d, oa, yr, gates, p3d, bf(woa), bf(wor), bf(wout), row(nffn), bf(wg), bf(wu), bf(wd),
      row(nple), bf(wple), bf(wpg), row(nfin))


def _tile(n, pref):
    t = min(n, pref)
    assert n % t == 0, (n, t)
    return t


def kernel(x, p, rel_bias, norm_mix, w_in, lam_q1, lam_k1, lam_q2, lam_k2, attn_subln, rwkv_mu, rwkv_w0, rwkv_w2, rwkv_a0, rwkv_a2, rwkv_g2, rwkv_kk, rwkv_ka, rwkv_rk, rwkv_lnx_w, rwkv_lnx_b, w_out_attn, w_out_rwkv, w_out, norm_ffn, w_ffn_gate, w_ffn_up, w_ffn_down, norm_ple, w_ple, w_ple_gate, norm_final):
    B, S, D = x.shape
    depth = w_in.shape[0]
    T = B * S
    t_attn = _tile(S, ATTN_TILE)
    tr_attn = t_attn // ATTN_ROW_SPLIT
    assert tr_attn % LANES == 0
    tm_in = _tile(T, 1024)
    tm_tok = _tile(T, 512)
    tm_prep = _tile(S, 512)
    tm_scan = _tile(S, 16 * SCAN_CHUNK)
    assert tm_scan % SCAN_CHUNK == 0

    bias_tiles = _bias_tiles(rel_bias, tr_attn)
    x2d = x.reshape(T, D)
    for i in range(depth):
        qkv, zr, gates = _inproj(x2d, norm_mix[i], w_in[i].astype(BF16), tm_in)

        lam_init = 0.8 - 0.6 * math.exp(-0.3 * i)
        lam = (jnp.exp(jnp.sum(lam_q1[i] * lam_k1[i])) - jnp.exp(jnp.sum(lam_q2[i] * lam_k2[i])) + lam_init)
        scal = jnp.stack([lam.astype(F32), jnp.asarray(1.0 - lam_init, F32)])
        oa = _attention(qkv.reshape(B, S, -1), bias_tiles, scal, attn_subln[i], t_attn,
                        max(1, min(ATTN_FAR_TILES, S // t_attn)))

        r, lw, k, v, a, b, g, bg = _rwkv_prep(zr.reshape(B, S, -1), rwkv_mu[i], rwkv_w0[i], rwkv_w2[i],
                                              rwkv_a0[i], rwkv_a2[i], rwkv_g2[i], rwkv_kk[i], rwkv_ka[i],
                                              rwkv_rk[i].reshape(-1), tm_prep)
        yr = _rwkv_scan(r, lw, k, v, a, b, g, bg, rwkv_lnx_w[i], rwkv_lnx_b[i], tm_scan)

        x2d = _post(x2d, oa.reshape(T, -1), yr.reshape(T, -1), gates, p.reshape(depth, T, -1), i,
                    w_out_attn[i], w_out_rwkv[i], w_out[i], norm_ffn[i], w_ffn_gate[i], w_ffn_up[i],
                    w_ffn_down[i], norm_ple[i], w_ple[i], w_ple_gate[i], norm_final,
                    final=(i == depth - 1), tm=tm_tok)
    return x2d.reshape(B, S, D)
```

```python
import functools
import math

import jax
import jax.numpy as jnp
from jax import lax
from jax.experimental import pallas as pl
from jax.experimental.pallas import tpu as pltpu

F32 = jnp.float32
BF16 = jnp.bfloat16

N_ATTN_HEADS = 4
ATTN_HALF_DIM = 64
ATTN_V_DIM = 128
ATTN_WIDTH = 512
N_RWKV_HEADS = 8
RWKV_HEAD_DIM = 64
RWKV_WIDTH = 512
DECAY_LORA = 64
AAA_LORA = 64
GATE_LORA = 128
RWKV_COLS = 3 * RWKV_WIDTH + DECAY_LORA + AAA_LORA + GATE_LORA
NUM_BUCKETS = 32
MAX_DISTANCE = 128
NORM_EPS = 1e-6
SUBLN_EPS = 1e-5
GN_EPS = 64e-5

LANES = 128
BF16_SUBLANES = 16
VMEM_LIMIT_BYTES = 56 * 1024 * 1024
ATTN_TILE = 1024
ATTN_FAR_TILES = 2
ATTN_ROW_SPLIT = 4
BIAS_REACH = 128
LOG2_E = math.log2(math.e)
SCAN_CHUNK = 64
MASK_VALUE = -1e30


def _cparams(*sem):
    return pltpu.CompilerParams(dimension_semantics=sem, vmem_limit_bytes=VMEM_LIMIT_BYTES)


def _const_spec(shape):
    nd = len(shape)
    return pl.BlockSpec(shape, lambda *_: (0,) * nd, pipeline_mode=pl.Buffered(1))


def _dot(a, b):
    return jnp.dot(a.astype(BF16), b.astype(BF16), preferred_element_type=F32)


def _dot_nt(a, b):
    return lax.dot_general(a.astype(BF16), b.astype(BF16), (((1,), (1,)), ((), ())),
                           preferred_element_type=F32)


def _dot_tn(a, b):
    return lax.dot_general(a.astype(BF16), b.astype(BF16), (((0,), (0,)), ((), ())),
                           preferred_element_type=F32)


def _dot_hilo(a_exact_bf16, x):
    hi = x.astype(BF16)
    lo = (x - hi.astype(F32)).astype(BF16)
    return (jnp.dot(a_exact_bf16, hi, preferred_element_type=F32)
            + jnp.dot(a_exact_bf16, lo, preferred_element_type=F32))


def _rms(x, g, eps):
    return x * lax.rsqrt(jnp.mean(x * x, axis=-1, keepdims=True) + eps) * g


def _inproj_kernel(x_ref, g_ref, w_ref, qkv_ref, zr_ref, gates_ref):
    h = _rms(x_ref[...], g_ref[...], NORM_EPS).astype(BF16)
    scale = ATTN_HALF_DIM ** -0.5 * LOG2_E
    nq = 3 * ATTN_WIDTH
    for c0 in range(0, nq, 512):
        z = jnp.dot(h, w_ref[:, c0:c0 + 512], preferred_element_type=F32)
        if c0 < ATTN_WIDTH:
            z = z * scale
        qkv_ref[:, c0:c0 + 512] = z.astype(BF16)
    c0 = nq
    while c0 < nq + RWKV_COLS:
        w = min(512, nq + RWKV_COLS - c0)
        zr_ref[:, c0 - nq:c0 - nq + w] = jnp.dot(h, w_ref[:, c0:c0 + w],
                                                 preferred_element_type=F32).astype(zr_ref.dtype)
        c0 += w
    base = nq + RWKV_COLS
    for c0 in range(0, gates_ref.shape[1], 512):
        z = jnp.dot(h, w_ref[:, base + c0:base + c0 + 512], preferred_element_type=F32)
        gates_ref[:, c0:c0 + 512] = jax.nn.sigmoid(z).astype(BF16)


def _inproj(x2d, g, w_bf16, tm):
    T, D = x2d.shape
    ncols = w_bf16.shape[1]
    ngate = ncols - 3 * ATTN_WIDTH - RWKV_COLS
    return pl.pallas_call(
        _inproj_kernel,
        grid=(T // tm,),
        in_specs=[pl.BlockSpec((tm, D), lambda i: (i, 0)),
                  _const_spec((1, D)),
                  _const_spec((D, ncols))],
        out_specs=[pl.BlockSpec((tm, 3 * ATTN_WIDTH), lambda i: (i, 0)),
                   pl.BlockSpec((tm, RWKV_COLS), lambda i: (i, 0)),
                   pl.BlockSpec((tm, ngate), lambda i: (i, 0))],
        out_shape=[jax.ShapeDtypeStruct((T, 3 * ATTN_WIDTH), BF16),
                   jax.ShapeDtypeStruct((T, RWKV_COLS), BF16),
                   jax.ShapeDtypeStruct((T, ngate), BF16)],
        compiler_params=_cparams("parallel"),
        name="inproj",
    )(x2d, g.reshape(1, D), w_bf16)


def _bias_tiles_kernel(rb_ref, out_ref):
    hc = pl.program_id(0)
    tr, bw = out_ref.shape[-2:]
    rows = lax.broadcasted_iota(jnp.int32, (tr, bw), 0)
    cols = lax.broadcasted_iota(jnp.int32, (tr, bw), 1)
    max_exact = NUM_BUCKETS // 2
    n_hc = 2 * N_ATTN_HEADS
    far = rb_ref[(NUM_BUCKETS - 1) * n_hc + hc]
    dist = rows + BIAS_REACH - cols
    n = jnp.maximum(dist, 0)
    nf = jnp.maximum(n, max_exact).astype(F32)
    large = max_exact + (jnp.log(nf / max_exact) / math.log(MAX_DISTANCE / max_exact)
                         * (NUM_BUCKETS - max_exact)).astype(jnp.int32)
    large = jnp.minimum(large, NUM_BUCKETS - 1)
    bucket = jnp.where(n < max_exact, n, large)
    tile = jnp.zeros((tr, bw), F32)
    for b in range(NUM_BUCKETS):
        tile = jnp.where(bucket == b, rb_ref[b * n_hc + hc], tile)
    tile = (tile - far) * LOG2_E
    out_ref[0] = jnp.where(dist >= 0, tile, MASK_VALUE)


def _bias_tiles(rel_bias, tr):
    n_hc = 2 * N_ATTN_HEADS
    bw = BIAS_REACH + tr
    return pl.pallas_call(
        _bias_tiles_kernel,
        grid=(n_hc,),
        in_specs=[pl.BlockSpec(memory_space=pltpu.SMEM)],
        out_specs=pl.BlockSpec((1, tr, bw), lambda i: (i, 0, 0)),
        out_shape=jax.ShapeDtypeStruct((n_hc, tr, bw), F32),
        compiler_params=_cparams("parallel"),
        name="bias_tiles",
    )(rel_bias.reshape(-1).astype(F32))


def _attn_kernel(scal_ref, q_ref, k_ref, v_ref, bias_ref, g_ref, o_ref, m_ref, acc_ref, v1_ref, *,
                 far_tiles, row_split):
    t = q_ref.shape[1]
    tr = t // row_split
    qi = pl.program_id(2)

    @pl.when(qi == 0)
    def _():
        v1_ref[:, 0:ATTN_V_DIM] = v_ref[0]
        v1_ref[:, ATTN_V_DIM:] = jnp.ones((v1_ref.shape[0], LANES), BF16)

    lane = lax.broadcasted_iota(jnp.int32, (1, LANES), 1)
    q = q_ref[0]
    zero = jnp.zeros_like(q)
    qm = (jnp.where(lane < ATTN_HALF_DIM, q, zero), jnp.where(lane >= ATTN_HALF_DIM, q, zero))

    m_ref[...] = jnp.full(m_ref.shape, MASK_VALUE, F32)
    acc_ref[...] = jnp.zeros(acc_ref.shape, F32)

    streams = [(c, r) for c in range(2) for r in range(row_split)]

    def step(start, width, kind):
        kb = k_ref[0, pl.ds(start, width), :]
        vb = v1_ref[pl.ds(start, width), :]
        def scores(c, r):
            w = width if kind is None else width - t + (r + 1) * tr
            s = lax.dot_general(qm[c][r * tr:(r + 1) * tr], kb[0:w], (((1,), (1,)), ((), ())),
                                preferred_element_type=F32)
            if kind is not None:
                bw = min(bias_ref.shape[-1], w)
                band = bias_ref[0, c, :, bias_ref.shape[-1] - bw:]
                s = s + band if w == bw else jnp.concatenate([s[:, :w - bw], s[:, w - bw:] + band], axis=1)
            return s

        def softmax_pv(c, r, s):
            rows = slice(r * tr, (r + 1) * tr)
            w = s.shape[1]
            m_prev = m_ref[c, rows, :]
            m_new = jnp.maximum(m_prev, jnp.max(s, axis=-1, keepdims=True))
            alpha = jnp.exp2(m_prev - m_new)
            p = jnp.exp2(s - jnp.concatenate([m_new] * (w // LANES), axis=1))
            acc_ref[c, rows, :] = (jnp.concatenate([alpha, alpha], axis=1) * acc_ref[c, rows, :]
                                   + jnp.dot(p.astype(BF16), vb[0:w], preferred_element_type=F32))
            m_ref[c, rows, :] = m_new

        pending = [scores(c, r) for c, r in streams]
        for (c, r), s in zip(streams, pending):
            softmax_pv(c, r, s)

    n_far = jnp.maximum(qi - 1, 0)
    far_w = far_tiles * t

    def far_body(j, carry):
        step(pl.multiple_of(j * far_w, far_w), far_w, None)
        return carry

    n_wide = n_far // far_tiles
    lax.fori_loop(0, n_wide, far_body, 0)
    pos = n_wide * far_tiles
    rem = n_far - pos
    w_tiles = far_tiles // 2
    while w_tiles >= 1:
        take = rem >= w_tiles

        @pl.when(take)
        def _(pos=pos, w_tiles=w_tiles):
            step(pl.multiple_of(pos * t, t), w_tiles * t, None)

        pos = pos + jnp.where(take, w_tiles, 0)
        rem = rem - jnp.where(take, w_tiles, 0)
        w_tiles //= 2

    if k_ref.shape[1] >= 2 * t:
        @pl.when(qi >= 1)
        def _():
            step(pl.multiple_of((qi - 1) * t, t), 2 * t, "near")

    @pl.when(qi == 0)
    def _():
        step(0, t, "first")

    lam = scal_ref[0]
    out_scale = scal_ref[1]
    dv = ATTN_V_DIM
    o = acc_ref[0, :, 0:dv] / acc_ref[0, :, dv:] - lam * (acc_ref[1, :, 0:dv] / acc_ref[1, :, dv:])
    o_ref[0] = (_rms(o, g_ref[...], SUBLN_EPS) * out_scale).astype(o_ref.dtype)


def _attention(qkv, bias_tiles, scal, subln_g, t, far_tiles):
    B, S, _ = qkv.shape
    H = N_ATTN_HEADS
    return pl.pallas_call(
        functools.partial(_attn_kernel, far_tiles=far_tiles, row_split=ATTN_ROW_SPLIT),
        grid=(B, H, S // t),
        in_specs=[pl.BlockSpec(memory_space=pltpu.SMEM),
                  pl.BlockSpec((1, t, LANES), lambda b, h, i: (b, i, h)),
                  pl.BlockSpec((1, S, LANES), lambda b, h, i: (b, 0, H + h)),
                  pl.BlockSpec((1, S, LANES), lambda b, h, i: (b, 0, 2 * H + h)),
                  pl.BlockSpec((1, 2) + bias_tiles.shape[-2:], lambda b, h, i: (h, 0, 0, 0)),
                  _const_spec((1, ATTN_V_DIM))],
        out_specs=pl.BlockSpec((1, t, LANES), lambda b, h, i: (b, i, h)),
        out_shape=jax.ShapeDtypeStruct((B, S, ATTN_WIDTH), BF16),
        scratch_shapes=[pltpu.VMEM((2, t, LANES), F32),
                        pltpu.VMEM((2, t, ATTN_V_DIM + LANES), F32),
                        pltpu.VMEM((S, ATTN_V_DIM + LANES), BF16)],
        compiler_params=_cparams("parallel", "parallel", "arbitrary"),
        name="diff_attention",
    )(scal, qkv, qkv, qkv, bias_tiles.reshape((H, 2) + bias_tiles.shape[-2:]), subln_g.reshape(1, ATTN_V_DIM))


def _rwkv_prep_kernel(z_ref, zp_ref, mu_ref, w0_ref, w2_ref, a0_ref, a2_ref, g2_ref, kk_ref, ka_ref,
                      rk_ref, seg_ref,
                      r_ref, lw_ref, k_ref, v_ref, a_ref, b_ref, g_ref, bg_ref):
    i = pl.program_id(1)
    z = z_ref[0].astype(F32)
    tm = z.shape[0]
    W = RWKV_WIDTH
    last = zp_ref.shape[1] - 1
    prev_row = zp_ref[0, last:last + 1, :].astype(F32) * (i > 0).astype(F32)
    row = lax.broadcasted_iota(jnp.int32, (tm, 1), 0)
    prev = jnp.where(row == 0, prev_row, pltpu.roll(z, 1, 0))
    zs = z + (prev - z) * mu_ref[...]
    r = zs[:, 0:W]
    kr = zs[:, W:2 * W]
    vr = zs[:, 2 * W:3 * W]
    xwa = zs[:, 3 * W:3 * W + DECAY_LORA + AAA_LORA]
    xg = zs[:, 3 * W + DECAY_LORA + AAA_LORA:]

    dw = w0_ref[...] + _dot(jnp.tanh(xwa), w2_ref[...])
    softplus = jnp.maximum(-dw, 0.0) + jnp.log(1.0 + jnp.exp(-jnp.abs(dw)))
    lw_ref[0] = -jnp.exp(-softplus - 0.5)
    asig = jax.nn.sigmoid(a0_ref[...] + _dot(xwa, a2_ref[...]))
    g = _dot(jax.nn.sigmoid(xg), g2_ref[...])

    seg = seg_ref[...]
    kk = kr * kk_ref[...]
    kkn = kk * jnp.minimum(lax.rsqrt(_dot(kk * kk, seg)), 1e12)
    kmod = kr * (1.0 + (asig - 1.0) * ka_ref[...])
    bonus = _dot(r * kmod * rk_ref[...], seg) * vr

    r_ref[0] = r.astype(r_ref.dtype)
    k_ref[0] = kmod.astype(k_ref.dtype)
    v_ref[0] = vr.astype(v_ref.dtype)
    a_ref[0] = (-kkn).astype(a_ref.dtype)
    b_ref[0] = (kkn * asig).astype(b_ref.dtype)
    g_ref[0] = g.astype(g_ref.dtype)
    bg_ref[0] = (bonus * g).astype(bg_ref.dtype)


def _rwkv_prep(zr, mu, w0, w2, a0, a2, g2, kkp, ka, rk, tm):
    B, S, C = zr.shape
    W = RWKV_WIDTH
    w2p = jnp.concatenate([w2, jnp.zeros_like(w2)], axis=0).astype(BF16)
    a2p = jnp.concatenate([jnp.zeros_like(a2), a2], axis=0).astype(BF16)
    head = jnp.arange(W) // RWKV_HEAD_DIM
    seg = (head[:, None] == head[None, :]).astype(BF16)
    row = lambda v: v.reshape(1, -1).astype(F32)
    sub = BF16_SUBLANES
    nblk = tm // sub
    out_spec = pl.BlockSpec((1, tm, W), lambda b, i: (b, i, 0))
    return pl.pallas_call(
        _rwkv_prep_kernel,
        grid=(B, S // tm),
        in_specs=[pl.BlockSpec((1, tm, C), lambda b, i: (b, i, 0)),
                  pl.BlockSpec((1, sub, C), lambda b, i: (b, jnp.maximum(i * nblk - 1, 0), 0)),
                  _const_spec((1, C)), _const_spec((1, W)), _const_spec((LANES, W)),
                  _const_spec((1, W)), _const_spec((LANES, W)), _const_spec((GATE_LORA, W)),
                  _const_spec((1, W)), _const_spec((1, W)), _const_spec((1, W)),
                  _const_spec((W, W))],
        out_specs=[out_spec] * 8,
        out_shape=[jax.ShapeDtypeStruct((B, S, W), F32 if n == 1 else BF16) for n in range(8)],
        compiler_params=_cparams("parallel", "parallel"),
        name="rwkv_prep",
    )(zr, zr, row(mu), row(w0), w2p, row(a0), a2p, g2.astype(BF16), row(kkp), row(ka), row(rk), seg)


def _scan_local(chunks, consts, tick):
    C = SCAN_CHUNK
    n = len(chunks)
    tril, strict4, incl4, bd = consts
    zc = jnp.zeros((C, LANES), F32)
    cat0 = lambda *xs: jnp.concatenate(xs, axis=0)
    cat1 = lambda *xs: jnp.concatenate(xs, axis=1)

    def by_head(x):
        lo = lax.broadcasted_iota(jnp.int32, (1, x.shape[1]), 1) % LANES < RWKV_HEAD_DIM
        return cat0(jnp.where(lo, x, 0.0), jnp.where(lo, 0.0, x))

    L_all = _dot_hilo(tril, cat1(*[ch[1] for ch in chunks]))
    pre = []
    for i, (r, lw, k, v, a, b) in enumerate(chunks):
        L = L_all[:, i * LANES:(i + 1) * LANES]
        winv = jnp.exp(-L)
        l_end = L[C - 1:C, :]
        wend = jnp.exp(l_end - L)
        rt = r * jnp.exp(L)
        at = a * jnp.exp(L - lw)
        bk4 = cat0(by_head(b * winv), by_head(k * winv))
        bk_end = cat0(b * wend, k * wend)
        pre.append((rt, at, bk4, bk_end, jnp.exp(l_end)))
    tick()

    outs = [_dot_nt(cat0(at, rt), bk4) for rt, at, bk4, _, _ in pre]
    tops = [jnp.where(strict4, o[0:C], 0.0) for o in outs]
    bots = [jnp.where(incl4, o[C:2 * C], 0.0) for o in outs]
    tick()

    aak_v = [_dot(top[:, 2 * C:], by_head(ch[3])) for top, ch in zip(tops, chunks)]
    tick()
    A = [top[:, :2 * C] for top in tops]
    Z = [cat1(p[1], av) for p, av in zip(pre, aak_v)]
    n_steps = int(math.log2(C))
    for s in range(n_steps):
        Z = [z + _dot(a_, by_head(z)) for a_, z in zip(A, Z)]
        tick()
        if s + 1 < n_steps:
            A = [_dot(a_, by_head(a_)) for a_ in A]
            tick()
    ahat = [z[:, :LANES] for z in Z]
    vhat = [z[:, LANES:] for z in Z]

    zc2 = jnp.zeros((2 * C, LANES), F32)
    yr2 = [_dot(bot, cat0(by_head(cat1(vh, ah)), cat1(by_head(ch[3]), zc2)))
           for bot, vh, ah, ch in zip(bots, vhat, ahat, chunks)]
    tick()
    qn = [_dot_tn(cat0(cat1(ah, vh), cat1(zc, ch[3])), p[3])
          for ah, vh, ch, p in zip(ahat, vhat, chunks, pre)]
    tick()
    res = []
    for i in range(n):
        yhat = yr2[i][:, :LANES]
        rhat = pre[i][0] + yr2[i][:, LANES:]
        Q = jnp.where(bd, qn[i][0:LANES], 0.0)
        Nt = jnp.where(bd, qn[i][LANES:], 0.0)
        res.append((rhat, yhat, Q, Nt, pre[i][4]))
    return res


SCAN_LOCAL_STAGES = 5 + 2 * int(math.log2(SCAN_CHUNK)) - 1


def _rwkv_scan_kernel(r_ref, lw_ref, k_ref, v_ref, a_ref, b_ref, g_ref, bg_ref, lnw_ref, lnb_ref,
                      o_ref, s_ref, rh_ref, yh_ref, q_ref, n_ref, wc_ref):
    C = SCAN_CHUNK
    n_chunks = r_ref.shape[1] // C
    i = pl.program_id(2)
    n_blocks = pl.num_programs(2) - 1

    @pl.when(i == 0)
    def _():
        s_ref[...] = jnp.zeros(s_ref.shape, F32)
        rh_ref[...] = jnp.zeros(rh_ref.shape, rh_ref.dtype)
        yh_ref[...] = jnp.zeros(yh_ref.shape, F32)
        q_ref[...] = jnp.zeros(q_ref.shape, q_ref.dtype)
        n_ref[...] = jnp.zeros(n_ref.shape, F32)
        wc_ref[...] = jnp.zeros(wc_ref.shape, F32)

    ri = lax.broadcasted_iota(jnp.int32, (C, 4 * C), 0)
    ci = lax.broadcasted_iota(jnp.int32, (C, 4 * C), 1) % C
    strict4 = ci < ri
    incl4 = ci <= ri
    lane = lax.broadcasted_iota(jnp.int32, (1, LANES), 1)
    lane_lo = lane < RWKV_HEAD_DIM
    lane_hi = lane >= RWKV_HEAD_DIM
    r128 = lax.broadcasted_iota(jnp.int32, (LANES, LANES), 0)
    c128 = lax.broadcasted_iota(jnp.int32, (LANES, LANES), 1)
    bd = (r128 < RWKV_HEAD_DIM) == (c128 < RWKV_HEAD_DIM)
    tril = (lax.broadcasted_iota(jnp.int32, (C, C), 1) <= lax.broadcasted_iota(jnp.int32, (C, C), 0)).astype(BF16)
    consts = (tril, strict4, incl4, bd)
    inv_n = 1.0 / RWKV_HEAD_DIM

    def head_sums(x):
        lo = jnp.sum(jnp.where(lane_lo, x, 0.0), axis=-1, keepdims=True)
        hi = jnp.sum(jnp.where(lane_hi, x, 0.0), axis=-1, keepdims=True)
        return jnp.where(lane_lo, lo, hi)

    def recurrence_step(c, S):
        sl = slice(c * C, (c + 1) * C)
        y = _dot_nt(rh_ref[c], S) + yh_ref[c]
        S = S * wc_ref[c] + _dot(S, q_ref[c]) + n_ref[c]
        mean = head_sums(y) * inv_n
        d = y - mean
        var = head_sums(d * d) * inv_n
        yn = d * lax.rsqrt(var + GN_EPS) * lnw_ref[...] + lnb_ref[...]
        o_ref[0, sl, :] = (yn * g_ref[0, sl, :].astype(F32) + bg_ref[0, sl, :].astype(F32)).astype(o_ref.dtype)
        return S

    @pl.when(i < n_blocks)
    def _():
        state = [s_ref[...]]
        done = [0]
        ticks = [0]

        def tick():
            ticks[0] += 1
            target = min(n_chunks, (ticks[0] * n_chunks + SCAN_LOCAL_STAGES - 1) // SCAN_LOCAL_STAGES)
            while done[0] < target:
                state[0] = recurrence_step(done[0], state[0])
                done[0] += 1

        chunks = []
        for c in range(n_chunks):
            sl = slice(c * C, (c + 1) * C)
            chunks.append(tuple(ref[0, sl, :].astype(F32)
                                for ref in (r_ref, lw_ref, k_ref, v_ref, a_ref, b_ref)))
        local = _scan_local(chunks, consts, tick)
        assert done[0] == n_chunks
        s_ref[...] = state[0]
        for c in range(n_chunks):
            rhat, yhat, Q, Nt, wc = local[c]
            rh_ref[c] = rhat.astype(rh_ref.dtype)
            yh_ref[c] = yhat
            q_ref[c] = Q.astype(q_ref.dtype)
            n_ref[c] = Nt
            wc_ref[c] = wc

    @pl.when(i == n_blocks)
    def _():
        S = s_ref[...]
        for c in range(n_chunks):
            S = recurrence_step(c, S)


def _rwkv_scan(r, lw, k, v, a, b, g, bg, lnw, lnb, tm):
    B, S, W = r.shape
    n_pairs = W // LANES
    n_blocks = S // tm
    n_chunks = tm // SCAN_CHUNK
    cur = pl.BlockSpec((1, tm, LANES), lambda bb, p, i: (bb, jnp.minimum(i, n_blocks - 1), p))
    prev = pl.BlockSpec((1, tm, LANES), lambda bb, p, i: (bb, jnp.maximum(i - 1, 0), p))
    vec = pl.BlockSpec((1, LANES), lambda bb, p, i: (0, p))
    return pl.pallas_call(
        _rwkv_scan_kernel,
        grid=(B, n_pairs, n_blocks + 1),
        in_specs=[cur] * 6 + [prev, prev, vec, vec],
        out_specs=prev,
        out_shape=jax.ShapeDtypeStruct((B, S, W), BF16),
        scratch_shapes=[pltpu.VMEM((LANES, LANES), F32),
                        pltpu.VMEM((n_chunks, SCAN_CHUNK, LANES), BF16),
                        pltpu.VMEM((n_chunks, SCAN_CHUNK, LANES), F32),
                        pltpu.VMEM((n_chunks, LANES, LANES), BF16),
                        pltpu.VMEM((n_chunks, LANES, LANES), F32),
                        pltpu.VMEM((n_chunks, 1, LANES), F32)],
        compiler_params=_cparams("parallel", "parallel", "arbitrary"),
        name="rwkv_scan",
    )(r, lw, k, v, a, b, g, bg, lnw.reshape(1, W).astype(F32), lnb.reshape(1, W).astype(F32))


def _post_kernel(x_ref, oa_ref, yr_ref, gates_ref, p_ref, woa_ref, wor_ref, wout_ref, nffn_ref,
                 wg_ref, wu_ref, wd_ref, nple_ref, wple_ref, wpg_ref, nfin_ref, o_ref, *, final, ff_chunk):
    D = x_ref.shape[1]
    ya = jnp.dot(oa_ref[...], woa_ref[...], preferred_element_type=F32)
    yr = jnp.dot(yr_ref[...], wor_ref[...], preferred_element_type=F32)
    m = gates_ref[:, 0:D].astype(F32) * ya + gates_ref[:, D:2 * D].astype(F32) * yr
    x = x_ref[...] + _dot(m, wout_ref[...])

    h2 = _rms(x, nffn_ref[...], NORM_EPS).astype(BF16)
    d_ff = wg_ref.shape[1]
    acc = jnp.zeros_like(x)
    for c0 in range(0, d_ff, ff_chunk):
        gt = jnp.dot(h2, wg_ref[:, c0:c0 + ff_chunk], preferred_element_type=F32)
        up = jnp.dot(h2, wu_ref[:, c0:c0 + ff_chunk], preferred_element_type=F32)
        act = gt * jax.nn.sigmoid(gt) * up
        acc = acc + jnp.dot(act.astype(BF16), wd_ref[c0:c0 + ff_chunk, :], preferred_element_type=F32)
    x = x + acc

    e = _dot(p_ref[...], wple_ref[...])
    gp = jax.nn.sigmoid(_dot(_rms(x, nple_ref[...], NORM_EPS), wpg_ref[...]))
    x = x + gp * e
    if final:
        x = _rms(x, nfin_ref[...], NORM_EPS)
    o_ref[...] = x


def _post(x2d, oa, yr, gates, p3d, layer, woa, wor, wout, nffn, wg, wu, wd, nple, wple, wpg, nfin, final, tm):
    T, D = x2d.shape
    d_ff = wg.shape[1]
    ff_chunk = 256 if d_ff % 256 == 0 else LANES
    tok = lambda w: pl.BlockSpec((tm, w), lambda i: (i, 0))
    p_spec = pl.BlockSpec((None, tm, p3d.shape[2]), lambda i: (layer, i, 0))
    row = lambda v: v.reshape(1, -1).astype(F32)
    bf = lambda w: w.astype(BF16)
    return pl.pallas_call(
        functools.partial(_post_kernel, final=final, ff_chunk=ff_chunk),
        grid=(T // tm,),
        in_specs=[tok(D), tok(oa.shape[1]), tok(yr.shape[1]), tok(gates.shape[1]), p_spec,
                  _const_spec(woa.shape), _const_spec(wor.shape), _const_spec(wout.shape),
                  _const_spec((1, D)), _const_spec(wg.shape), _const_spec(wu.shape), _const_spec(wd.shape),
                  _const_spec((1, D)), _const_spec(wple.shape), _const_spec(wpg.shape), _const_spec((1, D))],
        out_specs=tok(D),
        out_shape=jax.ShapeDtypeStruct((T, D), F32),
        compiler_params=_cparams("parallel"),
        name="post",
    )(x2d, oa, yr, gates, p3d, bf(woa), bf(wor), bf(wout), row(nffn), bf(wg), bf(wu), bf(wd),
      row(nple), bf(wple), bf(wpg), row(nfin))


def _tile(n, pref):
    t = min(n, pref)
    assert n % t == 0, (n, t)
    return t


def kernel(x, p, rel_bias, norm_mix, w_in, lam_q1, lam_k1, lam_q2, lam_k2, attn_subln, rwkv_mu, rwkv_w0, rwkv_w2, rwkv_a0, rwkv_a2, rwkv_g2, rwkv_kk, rwkv_ka, rwkv_rk, rwkv_lnx_w, rwkv_lnx_b, w_out_attn, w_out_rwkv, w_out, norm_ffn, w_ffn_gate, w_ffn_up, w_ffn_down, norm_ple, w_ple, w_ple_gate, norm_final):
    B, S, D = x.shape
    depth = w_in.shape[0]
    T = B * S
    t_attn = _tile(S, ATTN_TILE)
    tr_attn = t_attn // ATTN_ROW_SPLIT
    assert tr_attn % LANES == 0
    tm_in = _tile(T, 1024)
    tm_tok = _tile(T, 512)
    tm_prep = _tile(S, 512)
    tm_scan = _tile(S, 16 * SCAN_CHUNK)
    assert tm_scan % SCAN_CHUNK == 0

    bias_tiles = _bias_tiles(rel_bias, tr_attn)
    x2d = x.reshape(T, D)
    for i in range(depth):
        qkv, zr, gates = _inproj(x2d, norm_mix[i], w_in[i].astype(BF16), tm_in)

        lam_init = 0.8 - 0.6 * math.exp(-0.3 * i)
        lam = (jnp.exp(jnp.sum(lam_q1[i] * lam_k1[i])) - jnp.exp(jnp.sum(lam_q2[i] * lam_k2[i])) + lam_init)
        scal = jnp.stack([lam.astype(F32), jnp.asarray(1.0 - lam_init, F32)])
        oa = _attention(qkv.reshape(B, S, -1), bias_tiles, scal, attn_subln[i], t_attn,
                        max(1, min(ATTN_FAR_TILES, S // t_attn)))

        r, lw, k, v, a, b, g, bg = _rwkv_prep(zr.reshape(B, S, -1), rwkv_mu[i], rwkv_w0[i], rwkv_w2[i],
                                              rwkv_a0[i], rwkv_a2[i], rwkv_g2[i], rwkv_kk[i], rwkv_ka[i],
                                              rwkv_rk[i].reshape(-1), tm_prep)
        yr = _rwkv_scan(r, lw, k, v, a, b, g, bg, rwkv_lnx_w[i], rwkv_lnx_b[i], tm_scan)

        x2d = _post(x2d, oa.reshape(T, -1), yr.reshape(T, -1), gates, p.reshape(depth, T, -1), i,
                    w_out_attn[i], w_out_rwkv[i], w_out[i], norm_ffn[i], w_ffn_gate[i], w_ffn_up[i],
                    w_ffn_down[i], norm_ple[i], w_ple[i], w_ple_gate[i], norm_final,
                    final=(i == depth - 1), tm=tm_tok)
    return x2d.reshape(B, S, D)
```

```python
import functools
import math

import jax
import jax.numpy as jnp
from jax import lax
from jax.experimental import pallas as pl
from jax.experimental.pallas import tpu as pltpu

F32 = jnp.float32
BF16 = jnp.bfloat16

N_ATTN_HEADS = 4
ATTN_HALF_DIM = 64
ATTN_V_DIM = 128
ATTN_WIDTH = 512
N_RWKV_HEADS = 8
RWKV_HEAD_DIM = 64
RWKV_WIDTH = 512
DECAY_LORA = 64
AAA_LORA = 64
GATE_LORA = 128
RWKV_COLS = 3 * RWKV_WIDTH + DECAY_LORA + AAA_LORA + GATE_LORA
NUM_BUCKETS = 32
MAX_DISTANCE = 128
NORM_EPS = 1e-6
SUBLN_EPS = 1e-5
GN_EPS = 64e-5

LANES = 128
BF16_SUBLANES = 16
VMEM_LIMIT_BYTES = 56 * 1024 * 1024
ATTN_TILE = 1024
ATTN_FAR_TILES = 2
ATTN_ROW_SPLIT = 4
BIAS_REACH = 128
LOG2_E = math.log2(math.e)
SCAN_CHUNK = 64
MASK_VALUE = -1e30


def _cparams(*sem):
    return pltpu.CompilerParams(dimension_semantics=sem, vmem_limit_bytes=VMEM_LIMIT_BYTES)


def _const_spec(shape):
    nd = len(shape)
    return pl.BlockSpec(shape, lambda *_: (0,) * nd, pipeline_mode=pl.Buffered(1))


def _dot(a, b):
    return jnp.dot(a.astype(BF16), b.astype(BF16), preferred_element_type=F32)


def _dot_nt(a, b):
    return lax.dot_general(a.astype(BF16), b.astype(BF16), (((1,), (1,)), ((), ())),
                           preferred_element_type=F32)


def _dot_tn(a, b):
    return lax.dot_general(a.astype(BF16), b.astype(BF16), (((0,), (0,)), ((), ())),
                           preferred_element_type=F32)


def _dot_hilo(a_exact_bf16, x):
    hi = x.astype(BF16)
    lo = (x - hi.astype(F32)).astype(BF16)
    return (jnp.dot(a_exact_bf16, hi, preferred_element_type=F32)
            + jnp.dot(a_exact_bf16, lo, preferred_element_type=F32))


def _rms(x, g, eps):
    return x * lax.rsqrt(jnp.mean(x * x, axis=-1, keepdims=True) + eps) * g


def _inproj_kernel(x_ref, g_ref, w_ref, qkv_ref, zr_ref, gates_ref):
    h = _rms(x_ref[...], g_ref[...], NORM_EPS).astype(BF16)
    scale = ATTN_HALF_DIM ** -0.5 * LOG2_E
    nq = 3 * ATTN_WIDTH
    for c0 in range(0, nq, 512):
        z = jnp.dot(h, w_ref[:, c0:c0 + 512], preferred_element_type=F32)
        if c0 < ATTN_WIDTH:
            z = z * scale
        qkv_ref[:, c0:c0 + 512] = z.astype(BF16)
    c0 = nq
    while c0 < nq + RWKV_COLS:
        w = min(512, nq + RWKV_COLS - c0)
        zr_ref[:, c0 - nq:c0 - nq + w] = jnp.dot(h, w_ref[:, c0:c0 + w],
                                                 preferred_element_type=F32).astype(zr_ref.dtype)
        c0 += w
    base = nq + RWKV_COLS
    for c0 in range(0, gates_ref.shape[1], 512):
        z = jnp.dot(h, w_ref[:, base + c0:base + c0 + 512], preferred_element_type=F32)
        gates_ref[:, c0:c0 + 512] = jax.nn.sigmoid(z).astype(BF16)


def _inproj(x2d, g, w_bf16, tm):
    T, D = x2d.shape
    ncols = w_bf16.shape[1]
    ngate = ncols - 3 * ATTN_WIDTH - RWKV_COLS
    return pl.pallas_call(
        _inproj_kernel,
        grid=(T // tm,),
        in_specs=[pl.BlockSpec((tm, D), lambda i: (i, 0)),
                  _const_spec((1, D)),
                  _const_spec((D, ncols))],
        out_specs=[pl.BlockSpec((tm, 3 * ATTN_WIDTH), lambda i: (i, 0)),
                   pl.BlockSpec((tm, RWKV_COLS), lambda i: (i, 0)),
                   pl.BlockSpec((tm, ngate), lambda i: (i, 0))],
        out_shape=[jax.ShapeDtypeStruct((T, 3 * ATTN_WIDTH), BF16),
                   jax.ShapeDtypeStruct((T, RWKV_COLS), BF16),
                   jax.ShapeDtypeStruct((T, ngate), BF16)],
        compiler_params=_cparams("parallel"),
        name="inproj",
    )(x2d, g.reshape(1, D), w_bf16)


def _bias_tiles_kernel(rb_ref, out_ref):
    hc = pl.program_id(0)
    tr, bw = out_ref.shape[-2:]
    rows = lax.broadcasted_iota(jnp.int32, (tr, bw), 0)
    cols = lax.broadcasted_iota(jnp.int32, (tr, bw), 1)
    max_exact = NUM_BUCKETS // 2
    n_hc = 2 * N_ATTN_HEADS
    far = rb_ref[(NUM_BUCKETS - 1) * n_hc + hc]
    dist = rows + BIAS_REACH - cols
    n = jnp.maximum(dist, 0)
    nf = jnp.maximum(n, max_exact).astype(F32)
    large = max_exact + (jnp.log(nf / max_exact) / math.log(MAX_DISTANCE / max_exact)
                         * (NUM_BUCKETS - max_exact)).astype(jnp.int32)
    large = jnp.minimum(large, NUM_BUCKETS - 1)
    bucket = jnp.where(n < max_exact, n, large)
    tile = jnp.zeros((tr, bw), F32)
    for b in range(NUM_BUCKETS):
        tile = jnp.where(bucket == b, rb_ref[b * n_hc + hc], tile)
    tile = (tile - far) * LOG2_E
    out_ref[0] = jnp.where(dist >= 0, tile, MASK_VALUE)


def _bias_tiles(rel_bias, tr):
    n_hc = 2 * N_ATTN_HEADS
    bw = BIAS_REACH + tr
    return pl.pallas_call(
        _bias_tiles_kernel,
        grid=(n_hc,),
        in_specs=[pl.BlockSpec(memory_space=pltpu.SMEM)],
        out_specs=pl.BlockSpec((1, tr, bw), lambda i: (i, 0, 0)),
        out_shape=jax.ShapeDtypeStruct((n_hc, tr, bw), F32),
        compiler_params=_cparams("parallel"),
        name="bias_tiles",
    )(rel_bias.reshape(-1).astype(F32))


def _attn_kernel(scal_ref, q_ref, k_ref, v_ref, bias_ref, g_ref, o_ref, m_ref, acc_ref, v1_ref, *,
                 far_tiles, row_split):
    t = q_ref.shape[1]
    tr = t // row_split
    qi = pl.program_id(2)

    @pl.when(qi == 0)
    def _():
        v1_ref[:, 0:ATTN_V_DIM] = v_ref[0]
        v1_ref[:, ATTN_V_DIM:] = jnp.ones((v1_ref.shape[0], LANES), BF16)

    lane = lax.broadcasted_iota(jnp.int32, (1, LANES), 1)
    q = q_ref[0]
    zero = jnp.zeros_like(q)
    qm = (jnp.where(lane < ATTN_HALF_DIM, q, zero), jnp.where(lane >= ATTN_HALF_DIM, q, zero))

    m_ref[...] = jnp.full(m_ref.shape, MASK_VALUE, F32)
    acc_ref[...] = jnp.zeros(acc_ref.shape, F32)

    streams = [(c, r) for c in range(2) for r in range(row_split)]

    def step(start, width, kind):
        kb = k_ref[0, pl.ds(start, width), :]
        vb = v1_ref[pl.ds(start, width), :]
        def scores(c, r):
            w = width if kind is None else width - t + (r + 1) * tr
            s = lax.dot_general(qm[c][r * tr:(r + 1) * tr], kb[0:w], (((1,), (1,)), ((), ())),
                                preferred_element_type=F32)
            if kind is not None:
                bw = min(bias_ref.shape[-1], w)
                band = bias_ref[0, c, :, bias_ref.shape[-1] - bw:]
                s = s + band if w == bw else jnp.concatenate([s[:, :w - bw], s[:, w - bw:] + band], axis=1)
            return s

        def softmax_pv(c, r, s):
            rows = slice(r * tr, (r + 1) * tr)
            w = s.shape[1]
            m_prev = m_ref[c, rows, :]
            m_new = jnp.maximum(m_prev, jnp.max(s, axis=-1, keepdims=True))
            alpha = jnp.exp2(m_prev - m_new)
            p = jnp.exp2(s - jnp.concatenate([m_new] * (w // LANES), axis=1))
            acc_ref[c, rows, :] = (jnp.concatenate([alpha, alpha], axis=1) * acc_ref[c, rows, :]
                                   + jnp.dot(p.astype(BF16), vb[0:w], preferred_element_type=F32))
            m_ref[c, rows, :] = m_new

        pending = [scores(c, r) for c, r in streams]
        for (c, r), s in zip(streams, pending):
            softmax_pv(c, r, s)

    n_far = jnp.maximum(qi - 1, 0)
    far_w = far_tiles * t

    def far_body(j, carry):
        step(pl.multiple_of(j * far_w, far_w), far_w, None)
        return carry

    n_wide = n_far // far_tiles
    lax.fori_loop(0, n_wide, far_body, 0)
    pos = n_wide * far_tiles
    rem = n_far - pos
    w_tiles = far_tiles // 2
    while w_tiles >= 1:
        take = rem >= w_tiles

        @pl.when(take)
        def _(pos=pos, w_tiles=w_tiles):
            step(pl.multiple_of(pos * t, t), w_tiles * t, None)

        pos = pos + jnp.where(take, w_tiles, 0)
        rem = rem - jnp.where(take, w_tiles, 0)
        w_tiles //= 2

    if k_ref.shape[1] >= 2 * t:
        @pl.when(qi >= 1)
        def _():
            step(pl.multiple_of((qi - 1) * t, t), 2 * t, "near")

    @pl.when(qi == 0)
    def _():
        step(0, t, "first")

    lam = scal_ref[0]
    out_scale = scal_ref[1]
    dv = ATTN_V_DIM
    o = acc_ref[0, :, 0:dv] / acc_ref[0, :, dv:] - lam * (acc_ref[1, :, 0:dv] / acc_ref[1, :, dv:])
    o_ref[0] = (_rms(o, g_ref[...], SUBLN_EPS) * out_scale).astype(o_ref.dtype)


def _attention(qkv, bias_tiles, scal, subln_g, t, far_tiles):
    B, S, _ = qkv.shape
    H = N_ATTN_HEADS
    return pl.pallas_call(
        functools.partial(_attn_kernel, far_tiles=far_tiles, row_split=ATTN_ROW_SPLIT),
        grid=(B, H, S // t),
        in_specs=[pl.BlockSpec(memory_space=pltpu.SMEM),
                  pl.BlockSpec((1, t, LANES), lambda b, h, i: (b, i, h)),
                  pl.BlockSpec((1, S, LANES), lambda b, h, i: (b, 0, H + h)),
                  pl.BlockSpec((1, S, LANES), lambda b, h, i: (b, 0, 2 * H + h)),
                  pl.BlockSpec((1, 2) + bias_tiles.shape[-2:], lambda b, h, i: (h, 0, 0, 0)),
                  _const_spec((1, ATTN_V_DIM))],
        out_specs=pl.BlockSpec((1, t, LANES), lambda b, h, i: (b, i, h)),
        out_shape=jax.ShapeDtypeStruct((B, S, ATTN_WIDTH), BF16),
        scratch_shapes=[pltpu.VMEM((2, t, LANES), F32),
                        pltpu.VMEM((2, t, ATTN_V_DIM + LANES), F32),
                        pltpu.VMEM((S, ATTN_V_DIM + LANES), BF16)],
        compiler_params=_cparams("parallel", "parallel", "arbitrary"),
        name="diff_attention",
    )(scal, qkv, qkv, qkv, bias_tiles.reshape((H, 2) + bias_tiles.shape[-2:]), subln_g.reshape(1, ATTN_V_DIM))


def _rwkv_prep_kernel(z_ref, zp_ref, mu_ref, w0_ref, w2_ref, a0_ref, a2_ref, g2_ref, kk_ref, ka_ref,
                      rk_ref, seg_ref,
                      r_ref, lw_ref, k_ref, v_ref, a_ref, b_ref, g_ref, bg_ref):
    i = pl.program_id(1)
    z = z_ref[0].astype(F32)
    tm = z.shape[0]
    W = RWKV_WIDTH
    last = zp_ref.shape[1] - 1
    prev_row = zp_ref[0, last:last + 1, :].astype(F32) * (i > 0).astype(F32)
    row = lax.broadcasted_iota(jnp.int32, (tm, 1), 0)
    prev = jnp.where(row == 0, prev_row, pltpu.roll(z, 1, 0))
    zs = z + (prev - z) * mu_ref[...]
    r = zs[:, 0:W]
    kr = zs[:, W:2 * W]
    vr = zs[:, 2 * W:3 * W]
    xwa = zs[:, 3 * W:3 * W + DECAY_LORA + AAA_LORA]
    xg = zs[:, 3 * W + DECAY_LORA + AAA_LORA:]

    dw = w0_ref[...] + _dot(jnp.tanh(xwa), w2_ref[...])
    softplus = jnp.maximum(-dw, 0.0) + jnp.log(1.0 + jnp.exp(-jnp.abs(dw)))
    lw_ref[0] = -jnp.exp(-softplus - 0.5)
    asig = jax.nn.sigmoid(a0_ref[...] + _dot(xwa, a2_ref[...]))
    g = _dot(jax.nn.sigmoid(xg), g2_ref[...])

    seg = seg_ref[...]
    kk = kr * kk_ref[...]
    kkn = kk * jnp.minimum(lax.rsqrt(_dot(kk * kk, seg)), 1e12)
    kmod = kr * (1.0 + (asig - 1.0) * ka_ref[...])
    bonus = _dot(r * kmod * rk_ref[...], seg) * vr

    r_ref[0] = r.astype(r_ref.dtype)
    k_ref[0] = kmod.astype(k_ref.dtype)
    v_ref[0] = vr.astype(v_ref.dtype)
    a_ref[0] = (-kkn).astype(a_ref.dtype)
    b_ref[0] = (kkn * asig).astype(b_ref.dtype)
    g_ref[0] = g.astype(g_ref.dtype)
    bg_ref[0] = (bonus * g).astype(bg_ref.dtype)


def _rwkv_prep(zr, mu, w0, w2, a0, a2, g2, kkp, ka, rk, tm):
    B, S, C = zr.shape
    W = RWKV_WIDTH
    w2p = jnp.concatenate([w2, jnp.zeros_like(w2)], axis=0).astype(BF16)
    a2p = jnp.concatenate([jnp.zeros_like(a2), a2], axis=0).astype(BF16)
    head = jnp.arange(W) // RWKV_HEAD_DIM
    seg = (head[:, None] == head[None, :]).astype(BF16)
    row = lambda v: v.reshape(1, -1).astype(F32)
    sub = BF16_SUBLANES
    nblk = tm // sub
    out_spec = pl.BlockSpec((1, tm, W), lambda b, i: (b, i, 0))
    return pl.pallas_call(
        _rwkv_prep_kernel,
        grid=(B, S // tm),
        in_specs=[pl.BlockSpec((1, tm, C), lambda b, i: (b, i, 0)),
                  pl.BlockSpec((1, sub, C), lambda b, i: (b, jnp.maximum(i * nblk - 1, 0), 0)),
                  _const_spec((1, C)), _const_spec((1, W)), _const_spec((LANES, W)),
                  _const_spec((1, W)), _const_spec((LANES, W)), _const_spec((GATE_LORA, W)),
                  _const_spec((1, W)), _const_spec((1, W)), _const_spec((1, W)),
                  _const_spec((W, W))],
        out_specs=[out_spec] * 8,
        out_shape=[jax.ShapeDtypeStruct((B, S, W), F32 if n == 1 else BF16) for n in range(8)],
        compiler_params=_cparams("parallel", "parallel"),
        name="rwkv_prep",
    )(zr, zr, row(mu), row(w0), w2p, row(a0), a2p, g2.astype(BF16), row(kkp), row(ka), row(rk), seg)


def _scan_local(chunks, consts, tick):
    C = SCAN_CHUNK
    n = len(chunks)
    tril, strict4, incl4, bd = consts
    zc = jnp.zeros((C, LANES), F32)
    cat0 = lambda *xs: jnp.concatenate(xs, axis=0)
    cat1 = lambda *xs: jnp.concatenate(xs, axis=1)

    def by_head(x):
        lo = lax.broadcasted_iota(jnp.int32, (1, x.shape[1]), 1) % LANES < RWKV_HEAD_DIM
        return cat0(jnp.where(lo, x, 0.0), jnp.where(lo, 0.0, x))

    L_all = _dot_hilo(tril, cat1(*[ch[1] for ch in chunks]))
    pre = []
    for i, (r, lw, k, v, a, b) in enumerate(chunks):
        L = L_all[:, i * LANES:(i + 1) * LANES]
        winv = jnp.exp(-L)
        l_end = L[C - 1:C, :]
        wend = jnp.exp(l_end - L)
        rt = r * jnp.exp(L)
        at = a * jnp.exp(L - lw)
        bk4 = cat0(by_head(b * winv), by_head(k * winv))
        bk_end = cat0(b * wend, k * wend)
        pre.append((rt, at, bk4, bk_end, jnp.exp(l_end)))
    tick()

    outs = [_dot_nt(cat0(at, rt), bk4) for rt, at, bk4, _, _ in pre]
    tops = [jnp.where(strict4, o[0:C], 0.0) for o in outs]
    bots = [jnp.where(incl4, o[C:2 * C], 0.0) for o in outs]
    tick()

    aak_v = [_dot(top[:, 2 * C:], by_head(ch[3])) for top, ch in zip(tops, chunks)]
    tick()
    A = [top[:, :2 * C] for top in tops]
    Z = [cat1(p[1], av) for p, av in zip(pre, aak_v)]
    n_steps = int(math.log2(C))
    for s in range(n_steps):
        Z = [z + _dot(a_, by_head(z)) for a_, z in zip(A, Z)]
        tick()
        if s + 1 < n_steps:
            A = [_dot(a_, by_head(a_)) for a_ in A]
            tick()
    ahat = [z[:, :LANES] for z in Z]
    vhat = [z[:, LANES:] for z in Z]

    zc2 = jnp.zeros((2 * C, LANES), F32)
    yr2 = [_dot(bot, cat0(by_head(cat1(vh, ah)), cat1(by_head(ch[3]), zc2)))
           for bot, vh, ah, ch in zip(bots, vhat, ahat, chunks)]
    tick()
    qn = [_dot_tn(cat0(cat1(ah, vh), cat1(zc, ch[3])), p[3])
          for ah, vh, ch, p in zip(ahat, vhat, chunks, pre)]
    tick()
    res = []
    for i in range(n):
        yhat = yr2[i][:, :LANES]
        rhat = pre[i][0] + yr2[i][:, LANES:]
        Q = jnp.where(bd, qn[i][0:LANES], 0.0)
        Nt = jnp.where(bd, qn[i][LANES:], 0.0)
        res.append((rhat, yhat, Q, Nt, pre[i][4]))
    return res


SCAN_LOCAL_STAGES = 5 + 2 * int(math.log2(SCAN_CHUNK)) - 1


def _rwkv_scan_kernel(r_ref, lw_ref, k_ref, v_ref, a_ref, b_ref, g_ref, bg_ref, lnw_ref, lnb_ref,
                      o_ref, s_ref, rh_ref, yh_ref, q_ref, n_ref, wc_ref, *, blocks_per_seq):
    C = SCAN_CHUNK
    n_chunks = r_ref.shape[1] // C
    i = pl.program_id(0)
    n_blocks = pl.num_programs(0) - 1
    prev_starts_seq = jnp.maximum(i - 1, 0) % blocks_per_seq == 0

    @pl.when(i == 0)
    def _():
        s_ref[...] = jnp.zeros(s_ref.shape, F32)
        rh_ref[...] = jnp.zeros(rh_ref.shape, rh_ref.dtype)
        yh_ref[...] = jnp.zeros(yh_ref.shape, F32)
        q_ref[...] = jnp.zeros(q_ref.shape, q_ref.dtype)
        n_ref[...] = jnp.zeros(n_ref.shape, F32)
        wc_ref[...] = jnp.zeros(wc_ref.shape, F32)

    ri = lax.broadcasted_iota(jnp.int32, (C, 4 * C), 0)
    ci = lax.broadcasted_iota(jnp.int32, (C, 4 * C), 1) % C
    strict4 = ci < ri
    incl4 = ci <= ri
    lane = lax.broadcasted_iota(jnp.int32, (1, LANES), 1)
    lane_lo = lane < RWKV_HEAD_DIM
    lane_hi = lane >= RWKV_HEAD_DIM
    r128 = lax.broadcasted_iota(jnp.int32, (LANES, LANES), 0)
    c128 = lax.broadcasted_iota(jnp.int32, (LANES, LANES), 1)
    bd = (r128 < RWKV_HEAD_DIM) == (c128 < RWKV_HEAD_DIM)
    tril = (lax.broadcasted_iota(jnp.int32, (C, C), 1) <= lax.broadcasted_iota(jnp.int32, (C, C), 0)).astype(BF16)
    consts = (tril, strict4, incl4, bd)
    inv_n = 1.0 / RWKV_HEAD_DIM

    def head_sums(x):
        lo = jnp.sum(jnp.where(lane_lo, x, 0.0), axis=-1, keepdims=True)
        hi = jnp.sum(jnp.where(lane_hi, x, 0.0), axis=-1, keepdims=True)
        return jnp.where(lane_lo, lo, hi)

    def recurrence_step(c, S):
        sl = slice(c * C, (c + 1) * C)
        y = _dot_nt(rh_ref[c], S) + yh_ref[c]
        S = S * wc_ref[c] + _dot(S, q_ref[c]) + n_ref[c]
        mean = head_sums(y) * inv_n
        d = y - mean
        var = head_sums(d * d) * inv_n
        yn = d * lax.rsqrt(var + GN_EPS) * lnw_ref[...] + lnb_ref[...]
        o_ref[0, sl, :] = (yn * g_ref[0, sl, :].astype(F32) + bg_ref[0, sl, :].astype(F32)).astype(o_ref.dtype)
        return S

    @pl.when(i < n_blocks)
    def _():
        state = [jnp.where(prev_starts_seq, 0.0, s_ref[...])]
        done = [0]
        ticks = [0]

        def tick():
            ticks[0] += 1
            target = min(n_chunks, (ticks[0] * n_chunks + SCAN_LOCAL_STAGES - 1) // SCAN_LOCAL_STAGES)
            while done[0] < target:
                state[0] = recurrence_step(done[0], state[0])
                done[0] += 1

        chunks = []
        for c in range(n_chunks):
            sl = slice(c * C, (c + 1) * C)
            chunks.append(tuple(ref[0, sl, :].astype(F32)
                                for ref in (r_ref, lw_ref, k_ref, v_ref, a_ref, b_ref)))
        local = _scan_local(chunks, consts, tick)
        assert done[0] == n_chunks
        s_ref[...] = state[0]
        for c in range(n_chunks):
            rhat, yhat, Q, Nt, wc = local[c]
            rh_ref[c] = rhat.astype(rh_ref.dtype)
            yh_ref[c] = yhat
            q_ref[c] = Q.astype(q_ref.dtype)
            n_ref[c] = Nt
            wc_ref[c] = wc

    @pl.when(i == n_blocks)
    def _():
        S = jnp.where(prev_starts_seq, 0.0, s_ref[...])
        for c in range(n_chunks):
            S = recurrence_step(c, S)


def _rwkv_scan(r, lw, k, v, a, b, g, bg, lnw, lnb, tm):
    B, S, W = r.shape
    n_pairs = W // LANES
    n_blocks = S // tm
    n_chunks = tm // SCAN_CHUNK
    total = B * n_pairs * n_blocks

    def block_of(j):
        return j // (n_pairs * n_blocks), j % n_blocks, (j // n_blocks) % n_pairs

    cur = pl.BlockSpec((1, tm, LANES), lambda i: block_of(jnp.minimum(i, total - 1)))
    prev = pl.BlockSpec((1, tm, LANES), lambda i: block_of(jnp.maximum(i - 1, 0)))
    vec = pl.BlockSpec((1, LANES), lambda i: (0, block_of(jnp.maximum(i - 1, 0))[2]))
    return pl.pallas_call(
        functools.partial(_rwkv_scan_kernel, blocks_per_seq=n_blocks),
        grid=(total + 1,),
        in_specs=[cur] * 6 + [prev, prev, vec, vec],
        out_specs=prev,
        out_shape=jax.ShapeDtypeStruct((B, S, W), BF16),
        scratch_shapes=[pltpu.VMEM((LANES, LANES), F32),
                        pltpu.VMEM((n_chunks, SCAN_CHUNK, LANES), BF16),
                        pltpu.VMEM((n_chunks, SCAN_CHUNK, LANES), F32),
                        pltpu.VMEM((n_chunks, LANES, LANES), BF16),
                        pltpu.VMEM((n_chunks, LANES, LANES), F32),
                        pltpu.VMEM((n_chunks, 1, LANES), F32)],
        compiler_params=_cparams("arbitrary"),
        name="rwkv_scan",
    )(r, lw, k, v, a, b, g, bg, lnw.reshape(1, W).astype(F32), lnb.reshape(1, W).astype(F32))


def _post_kernel(x_ref, oa_ref, yr_ref, gates_ref, p_ref, woa_ref, wor_ref, wout_ref, nffn_ref,
                 wg_ref, wu_ref, wd_ref, nple_ref, wple_ref, wpg_ref, nfin_ref, o_ref, *, final, ff_chunk):
    D = x_ref.shape[1]
    ya = jnp.dot(oa_ref[...], woa_ref[...], preferred_element_type=F32)
    yr = jnp.dot(yr_ref[...], wor_ref[...], preferred_element_type=F32)
    m = gates_ref[:, 0:D].astype(F32) * ya + gates_ref[:, D:2 * D].astype(F32) * yr
    x = x_ref[...] + _dot(m, wout_ref[...])

    h2 = _rms(x, nffn_ref[...], NORM_EPS).astype(BF16)
    d_ff = wg_ref.shape[1]
    acc = jnp.zeros_like(x)
    for c0 in range(0, d_ff, ff_chunk):
        gt = jnp.dot(h2, wg_ref[:, c0:c0 + ff_chunk], preferred_element_type=F32)
        up = jnp.dot(h2, wu_ref[:, c0:c0 + ff_chunk], preferred_element_type=F32)
        act = gt * jax.nn.sigmoid(gt) * up
        acc = acc + jnp.dot(act.astype(BF16), wd_ref[c0:c0 + ff_chunk, :], preferred_element_type=F32)
    x = x + acc

    e = _dot(p_ref[...], wple_ref[...])
    gp = jax.nn.sigmoid(_dot(_rms(x, nple_ref[...], NORM_EPS), wpg_ref[...]))
    x = x + gp * e
    if final:
        x = _rms(x, nfin_ref[...], NORM_EPS)
    o_ref[...] = x


def _post(x2d, oa, yr, gates, p3d, layer, woa, wor, wout, nffn, wg, wu, wd, nple, wple, wpg, nfin, final, tm):
    T, D = x2d.shape
    d_ff = wg.shape[1]
    ff_chunk = 256 if d_ff % 256 == 0 else LANES
    tok = lambda w: pl.BlockSpec((tm, w), lambda i: (i, 0))
    p_spec = pl.BlockSpec((None, tm, p3d.shape[2]), lambda i: (layer, i, 0))
    row = lambda v: v.reshape(1, -1).astype(F32)
    bf = lambda w: w.astype(BF16)
    return pl.pallas_call(
        functools.partial(_post_kernel, final=final, ff_chunk=ff_chunk),
        grid=(T // tm,),
        in_specs=[tok(D), tok(oa.shape[1]), tok(yr.shape[1]), tok(gates.shape[1]), p_spec,
                  _const_spec(woa.shape), _const_spec(wor.shape), _const_spec(wout.shape),
                  _const_spec((1, D)), _const_spec(wg.shape), _const_spec(wu.shape), _const_spec(wd.shape),
                  _const_spec((1, D)), _const_spec(wple.shape), _const_spec(wpg.shape), _const_spec((1, D))],
        out_specs=tok(D),
        out_shape=jax.ShapeDtypeStruct((T, D), F32),
        compiler_params=_cparams("parallel"),
        name="post",
    )(x2d, oa, yr, gates, p3d, bf(woa), bf(wor), bf(wout), row(nffn), bf(wg), bf(wu), bf(wd),
      row(nple), bf(wple), bf(wpg), row(nfin))


def _tile(n, pref):
    t = min(n, pref)
    assert n % t == 0, (n, t)
    return t


def kernel(x, p, rel_bias, norm_mix, w_in, lam_q1, lam_k1, lam_q2, lam_k2, attn_subln, rwkv_mu, rwkv_w0, rwkv_w2, rwkv_a0, rwkv_a2, rwkv_g2, rwkv_kk, rwkv_ka, rwkv_rk, rwkv_lnx_w, rwkv_lnx_b, w_out_attn, w_out_rwkv, w_out, norm_ffn, w_ffn_gate, w_ffn_up, w_ffn_down, norm_ple, w_ple, w_ple_gate, norm_final):
    B, S, D = x.shape
    depth = w_in.shape[0]
    T = B * S
    t_attn = _tile(S, ATTN_TILE)
    tr_attn = t_attn // ATTN_ROW_SPLIT
    assert tr_attn % LANES == 0
    tm_in = _tile(T, 1024)
    tm_tok = _tile(T, 512)
    tm_prep = _tile(S, 512)
    tm_scan = _tile(S, 16 * SCAN_CHUNK)
    assert tm_scan % SCAN_CHUNK == 0

    bias_tiles = _bias_tiles(rel_bias, tr_attn)
    x2d = x.reshape(T, D)
    for i in range(depth):
        qkv, zr, gates = _inproj(x2d, norm_mix[i], w_in[i].astype(BF16), tm_in)

        lam_init = 0.8 - 0.6 * math.exp(-0.3 * i)
        lam = (jnp.exp(jnp.sum(lam_q1[i] * lam_k1[i])) - jnp.exp(jnp.sum(lam_q2[i] * lam_k2[i])) + lam_init)
        scal = jnp.stack([lam.astype(F32), jnp.asarray(1.0 - lam_init, F32)])
        oa = _attention(qkv.reshape(B, S, -1), bias_tiles, scal, attn_subln[i], t_attn,
                        max(1, min(ATTN_FAR_TILES, S // t_attn)))

        r, lw, k, v, a, b, g, bg = _rwkv_prep(zr.reshape(B, S, -1), rwkv_mu[i], rwkv_w0[i], rwkv_w2[i],
                                              rwkv_a0[i], rwkv_a2[i], rwkv_g2[i], rwkv_kk[i], rwkv_ka[i],
                                              rwkv_rk[i].reshape(-1), tm_prep)
        yr = _rwkv_scan(r, lw, k, v, a, b, g, bg, rwkv_lnx_w[i], rwkv_lnx_b[i], tm_scan)

        x2d = _post(x2d, oa.reshape(T, -1), yr.reshape(T, -1), gates, p.reshape(depth, T, -1), i,
                    w_out_attn[i], w_out_rwkv[i], w_out[i], norm_ffn[i], w_ffn_gate[i], w_ffn_up[i],
                    w_ffn_down[i], norm_ple[i], w_ple[i], w_ple_gate[i], norm_final,
                    final=(i == depth - 1), tm=tm_tok)
    return x2d.reshape(B, S, D)
```

```python
import functools
import math

import jax
import jax.numpy as jnp
from jax import lax
from jax.experimental import pallas as pl
from jax.experimental.pallas import tpu as pltpu

F32 = jnp.float32
BF16 = jnp.bfloat16

N_ATTN_HEADS = 4
ATTN_HALF_DIM = 64
ATTN_V_DIM = 128
ATTN_WIDTH = 512
N_RWKV_HEADS = 8
RWKV_HEAD_DIM = 64
RWKV_WIDTH = 512
DECAY_LORA = 64
AAA_LORA = 64
GATE_LORA = 128
RWKV_COLS = 3 * RWKV_WIDTH + DECAY_LORA + AAA_LORA + GATE_LORA
NUM_BUCKETS = 32
MAX_DISTANCE = 128
NORM_EPS = 1e-6
SUBLN_EPS = 1e-5
GN_EPS = 64e-5

LANES = 128
BF16_SUBLANES = 16
VMEM_LIMIT_BYTES = 56 * 1024 * 1024
ATTN_TILE = 1024
ATTN_FAR_TILES = 2
ATTN_ROW_SPLIT = 4
BIAS_REACH = 128
LOG2_E = math.log2(math.e)
SCAN_CHUNK = 64
MASK_VALUE = -1e30


def _cparams(*sem):
    return pltpu.CompilerParams(dimension_semantics=sem, vmem_limit_bytes=VMEM_LIMIT_BYTES)


def _const_spec(shape):
    nd = len(shape)
    return pl.BlockSpec(shape, lambda *_: (0,) * nd, pipeline_mode=pl.Buffered(1))


def _dot(a, b):
    return jnp.dot(a.astype(BF16), b.astype(BF16), preferred_element_type=F32)


def _dot_nt(a, b):
    return lax.dot_general(a.astype(BF16), b.astype(BF16), (((1,), (1,)), ((), ())),
                           preferred_element_type=F32)


def _dot_tn(a, b):
    return lax.dot_general(a.astype(BF16), b.astype(BF16), (((0,), (0,)), ((), ())),
                           preferred_element_type=F32)


def _dot_hilo(a_exact_bf16, x):
    hi = x.astype(BF16)
    lo = (x - hi.astype(F32)).astype(BF16)
    return (jnp.dot(a_exact_bf16, hi, preferred_element_type=F32)
            + jnp.dot(a_exact_bf16, lo, preferred_element_type=F32))


def _rms(x, g, eps):
    return x * lax.rsqrt(jnp.mean(x * x, axis=-1, keepdims=True) + eps) * g


def _inproj_kernel(x_ref, g_ref, w_ref, qkv_ref, zr_ref, gates_ref):
    h = _rms(x_ref[...], g_ref[...], NORM_EPS).astype(BF16)
    scale = ATTN_HALF_DIM ** -0.5 * LOG2_E
    nq = 3 * ATTN_WIDTH
    for c0 in range(0, nq, 512):
        z = jnp.dot(h, w_ref[:, c0:c0 + 512], preferred_element_type=F32)
        if c0 < ATTN_WIDTH:
            z = z * scale
        qkv_ref[:, c0:c0 + 512] = z.astype(BF16)
    c0 = nq
    while c0 < nq + RWKV_COLS:
        w = min(512, nq + RWKV_COLS - c0)
        zr_ref[:, c0 - nq:c0 - nq + w] = jnp.dot(h, w_ref[:, c0:c0 + w],
                                                 preferred_element_type=F32).astype(zr_ref.dtype)
        c0 += w
    base = nq + RWKV_COLS
    for c0 in range(0, gates_ref.shape[1], 512):
        z = jnp.dot(h, w_ref[:, base + c0:base + c0 + 512], preferred_element_type=F32)
        gates_ref[:, c0:c0 + 512] = jax.nn.sigmoid(z).astype(BF16)


def _inproj(x2d, g, w_bf16, tm):
    T, D = x2d.shape
    ncols = w_bf16.shape[1]
    ngate = ncols - 3 * ATTN_WIDTH - RWKV_COLS
    return pl.pallas_call(
        _inproj_kernel,
        grid=(T // tm,),
        in_specs=[pl.BlockSpec((tm, D), lambda i: (i, 0)),
                  _const_spec((1, D)),
                  _const_spec((D, ncols))],
        out_specs=[pl.BlockSpec((tm, 3 * ATTN_WIDTH), lambda i: (i, 0)),
                   pl.BlockSpec((tm, RWKV_COLS), lambda i: (i, 0)),
                   pl.BlockSpec((tm, ngate), lambda i: (i, 0))],
        out_shape=[jax.ShapeDtypeStruct((T, 3 * ATTN_WIDTH), BF16),
                   jax.ShapeDtypeStruct((T, RWKV_COLS), BF16),
                   jax.ShapeDtypeStruct((T, ngate), BF16)],
        compiler_params=_cparams("parallel"),
        name="inproj",
    )(x2d, g.reshape(1, D), w_bf16)


def _bias_tiles_kernel(rb_ref, out_ref):
    hc = pl.program_id(0)
    tr, bw = out_ref.shape[-2:]
    rows = lax.broadcasted_iota(jnp.int32, (tr, bw), 0)
    cols = lax.broadcasted_iota(jnp.int32, (tr, bw), 1)
    max_exact = NUM_BUCKETS // 2
    n_hc = 2 * N_ATTN_HEADS
    far = rb_ref[(NUM_BUCKETS - 1) * n_hc + hc]
    dist = rows + BIAS_REACH - cols
    n = jnp.maximum(dist, 0)
    nf = jnp.maximum(n, max_exact).astype(F32)
    large = max_exact + (jnp.log(nf / max_exact) / math.log(MAX_DISTANCE / max_exact)
                         * (NUM_BUCKETS - max_exact)).astype(jnp.int32)
    large = jnp.minimum(large, NUM_BUCKETS - 1)
    bucket = jnp.where(n < max_exact, n, large)
    tile = jnp.zeros((tr, bw), F32)
    for b in range(NUM_BUCKETS):
        tile = jnp.where(bucket == b, rb_ref[b * n_hc + hc], tile)
    tile = (tile - far) * LOG2_E
    out_ref[0] = jnp.where(dist >= 0, tile, MASK_VALUE)


def _bias_tiles(rel_bias, tr):
    n_hc = 2 * N_ATTN_HEADS
    bw = BIAS_REACH + tr
    return pl.pallas_call(
        _bias_tiles_kernel,
        grid=(n_hc,),
        in_specs=[pl.BlockSpec(memory_space=pltpu.SMEM)],
        out_specs=pl.BlockSpec((1, tr, bw), lambda i: (i, 0, 0)),
        out_shape=jax.ShapeDtypeStruct((n_hc, tr, bw), F32),
        compiler_params=_cparams("parallel"),
        name="bias_tiles",
    )(rel_bias.reshape(-1).astype(F32))


def _attn_kernel(scal_ref, q_ref, k_ref, v_ref, bias_ref, g_ref, o_ref, m_ref, acc_ref, v1_ref, *,
                 far_tiles, row_split):
    t = q_ref.shape[1]
    tr = t // row_split
    qi = pl.program_id(2)

    @pl.when(qi == 0)
    def _():
        v1_ref[:, 0:ATTN_V_DIM] = v_ref[0]
        v1_ref[:, ATTN_V_DIM:] = jnp.ones((v1_ref.shape[0], LANES), BF16)

    lane = lax.broadcasted_iota(jnp.int32, (1, LANES), 1)
    q = q_ref[0]
    zero = jnp.zeros_like(q)
    qm = (jnp.where(lane < ATTN_HALF_DIM, q, zero), jnp.where(lane >= ATTN_HALF_DIM, q, zero))

    m_ref[...] = jnp.full(m_ref.shape, MASK_VALUE, F32)
    acc_ref[...] = jnp.zeros(acc_ref.shape, F32)

    streams = [(c, r) for c in range(2) for r in range(row_split)]

    def step(start, width, kind):
        kb = k_ref[0, pl.ds(start, width), :]
        vb = v1_ref[pl.ds(start, width), :]
        def scores(c, r):
            w = width if kind is None else width - t + (r + 1) * tr
            s = lax.dot_general(qm[c][r * tr:(r + 1) * tr], kb[0:w], (((1,), (1,)), ((), ())),
                                preferred_element_type=F32)
            if kind is not None:
                bw = min(bias_ref.shape[-1], w)
                band = bias_ref[0, c, :, bias_ref.shape[-1] - bw:]
                s = s + band if w == bw else jnp.concatenate([s[:, :w - bw], s[:, w - bw:] + band], axis=1)
            return s

        def softmax_pv(c, r, s):
            rows = slice(r * tr, (r + 1) * tr)
            w = s.shape[1]
            m_prev = m_ref[c, rows, :]
            m_new = jnp.maximum(m_prev, jnp.max(s, axis=-1, keepdims=True))
            alpha = jnp.exp2(m_prev - m_new)
            p = jnp.exp2(s - jnp.concatenate([m_new] * (w // LANES), axis=1))
            acc_ref[c, rows, :] = (jnp.concatenate([alpha, alpha], axis=1) * acc_ref[c, rows, :]
                                   + jnp.dot(p.astype(BF16), vb[0:w], preferred_element_type=F32))
            m_ref[c, rows, :] = m_new

        pending = [scores(c, r) for c, r in streams]
        for (c, r), s in zip(streams, pending):
            softmax_pv(c, r, s)

    n_far = jnp.maximum(qi - 1, 0)
    far_w = far_tiles * t

    def far_body(j, carry):
        step(pl.multiple_of(j * far_w, far_w), far_w, None)
        return carry

    n_wide = n_far // far_tiles
    lax.fori_loop(0, n_wide, far_body, 0)
    pos = n_wide * far_tiles
    rem = n_far - pos
    w_tiles = far_tiles // 2
    while w_tiles >= 1:
        take = rem >= w_tiles

        @pl.when(take)
        def _(pos=pos, w_tiles=w_tiles):
            step(pl.multiple_of(pos * t, t), w_tiles * t, None)

        pos = pos + jnp.where(take, w_tiles, 0)
        rem = rem - jnp.where(take, w_tiles, 0)
        w_tiles //= 2

    if k_ref.shape[1] >= 2 * t:
        @pl.when(qi >= 1)
        def _():
            step(pl.multiple_of((qi - 1) * t, t), 2 * t, "near")

    @pl.when(qi == 0)
    def _():
        step(0, t, "first")

    lam = scal_ref[0]
    out_scale = scal_ref[1]
    dv = ATTN_V_DIM
    o = acc_ref[0, :, 0:dv] / acc_ref[0, :, dv:] - lam * (acc_ref[1, :, 0:dv] / acc_ref[1, :, dv:])
    o_ref[0] = (_rms(o, g_ref[...], SUBLN_EPS) * out_scale).astype(o_ref.dtype)


def _attention(qkv, bias_tiles, scal, subln_g, t, far_tiles):
    B, S, _ = qkv.shape
    H = N_ATTN_HEADS
    return pl.pallas_call(
        functools.partial(_attn_kernel, far_tiles=far_tiles, row_split=ATTN_ROW_SPLIT),
        grid=(B, H, S // t),
        in_specs=[pl.BlockSpec(memory_space=pltpu.SMEM),
                  pl.BlockSpec((1, t, LANES), lambda b, h, i: (b, i, h)),
                  pl.BlockSpec((1, S, LANES), lambda b, h, i: (b, 0, H + h)),
                  pl.BlockSpec((1, S, LANES), lambda b, h, i: (b, 0, 2 * H + h)),
                  pl.BlockSpec((1, 2) + bias_tiles.shape[-2:], lambda b, h, i: (h, 0, 0, 0)),
                  _const_spec((1, ATTN_V_DIM))],
        out_specs=pl.BlockSpec((1, t, LANES), lambda b, h, i: (b, i, h)),
        out_shape=jax.ShapeDtypeStruct((B, S, ATTN_WIDTH), BF16),
        scratch_shapes=[pltpu.VMEM((2, t, LANES), F32),
                        pltpu.VMEM((2, t, ATTN_V_DIM + LANES), F32),
                        pltpu.VMEM((S, ATTN_V_DIM + LANES), BF16)],
        compiler_params=_cparams("parallel", "parallel", "arbitrary"),
        name="diff_attention",
    )(scal, qkv, qkv, qkv, bias_tiles.reshape((H, 2) + bias_tiles.shape[-2:]), subln_g.reshape(1, ATTN_V_DIM))


def _rwkv_prep_kernel(z_ref, zp_ref, mu_ref, w0_ref, w2_ref, a0_ref, a2_ref, g2_ref, kk_ref, ka_ref,
                      rk_ref, seg_ref,
                      r_ref, lw_ref, k_ref, v_ref, a_ref, b_ref, g_ref, bg_ref):
    i = pl.program_id(1)
    z = z_ref[0].astype(F32)
    tm = z.shape[0]
    W = RWKV_WIDTH
    last = zp_ref.shape[1] - 1
    prev_row = zp_ref[0, last:last + 1, :].astype(F32) * (i > 0).astype(F32)
    row = lax.broadcasted_iota(jnp.int32, (tm, 1), 0)
    prev = jnp.where(row == 0, prev_row, pltpu.roll(z, 1, 0))
    zs = z + (prev - z) * mu_ref[...]
    r = zs[:, 0:W]
    kr = zs[:, W:2 * W]
    vr = zs[:, 2 * W:3 * W]
    xwa = zs[:, 3 * W:3 * W + DECAY_LORA + AAA_LORA]
    xg = zs[:, 3 * W + DECAY_LORA + AAA_LORA:]

    dw = w0_ref[...] + _dot(jnp.tanh(xwa), w2_ref[...])
    softplus = jnp.maximum(-dw, 0.0) + jnp.log(1.0 + jnp.exp(-jnp.abs(dw)))
    lw_ref[0] = -jnp.exp(-softplus - 0.5)
    asig = jax.nn.sigmoid(a0_ref[...] + _dot(xwa, a2_ref[...]))
    g = _dot(jax.nn.sigmoid(xg), g2_ref[...])

    seg = seg_ref[...]
    kk = kr * kk_ref[...]
    kkn = kk * jnp.minimum(lax.rsqrt(_dot(kk * kk, seg)), 1e12)
    kmod = kr * (1.0 + (asig - 1.0) * ka_ref[...])
    bonus = _dot(r * kmod * rk_ref[...], seg) * vr

    r_ref[0] = r.astype(r_ref.dtype)
    k_ref[0] = kmod.astype(k_ref.dtype)
    v_ref[0] = vr.astype(v_ref.dtype)
    a_ref[0] = (-kkn).astype(a_ref.dtype)
    b_ref[0] = (kkn * asig).astype(b_ref.dtype)
    g_ref[0] = g.astype(g_ref.dtype)
    bg_ref[0] = (bonus * g).astype(bg_ref.dtype)


def _rwkv_prep(zr, mu, w0, w2, a0, a2, g2, kkp, ka, rk, tm):
    B, S, C = zr.shape
    W = RWKV_WIDTH
    w2p = jnp.concatenate([w2, jnp.zeros_like(w2)], axis=0).astype(BF16)
    a2p = jnp.concatenate([jnp.zeros_like(a2), a2], axis=0).astype(BF16)
    head = jnp.arange(W) // RWKV_HEAD_DIM
    seg = (head[:, None] == head[None, :]).astype(BF16)
    row = lambda v: v.reshape(1, -1).astype(F32)
    sub = BF16_SUBLANES
    nblk = tm // sub
    out_spec = pl.BlockSpec((1, tm, W), lambda b, i: (b, i, 0))
    return pl.pallas_call(
        _rwkv_prep_kernel,
        grid=(B, S // tm),
        in_specs=[pl.BlockSpec((1, tm, C), lambda b, i: (b, i, 0)),
                  pl.BlockSpec((1, sub, C), lambda b, i: (b, jnp.maximum(i * nblk - 1, 0), 0)),
                  _const_spec((1, C)), _const_spec((1, W)), _const_spec((LANES, W)),
                  _const_spec((1, W)), _const_spec((LANES, W)), _const_spec((GATE_LORA, W)),
                  _const_spec((1, W)), _const_spec((1, W)), _const_spec((1, W)),
                  _const_spec((W, W))],
        out_specs=[out_spec] * 8,
        out_shape=[jax.ShapeDtypeStruct((B, S, W), F32 if n == 1 else BF16) for n in range(8)],
        compiler_params=_cparams("parallel", "parallel"),
        name="rwkv_prep",
    )(zr, zr, row(mu), row(w0), w2p, row(a0), a2p, g2.astype(BF16), row(kkp), row(ka), row(rk), seg)


def _scan_local(chunks, consts, tick):
    C = SCAN_CHUNK
    n = len(chunks)
    tril, strict4, incl4, bd = consts
    zc = jnp.zeros((C, LANES), F32)
    cat0 = lambda *xs: jnp.concatenate(xs, axis=0)
    cat1 = lambda *xs: jnp.concatenate(xs, axis=1)

    def by_head(x):
        lo = lax.broadcasted_iota(jnp.int32, (1, x.shape[1]), 1) % LANES < RWKV_HEAD_DIM
        return cat0(jnp.where(lo, x, 0.0), jnp.where(lo, 0.0, x))

    L_all = _dot_hilo(tril, cat1(*[ch[1] for ch in chunks]))
    pre = []
    for i, (r, lw, k, v, a, b) in enumerate(chunks):
        L = L_all[:, i * LANES:(i + 1) * LANES]
        winv = jnp.exp(-L)
        l_end = L[C - 1:C, :]
        wend = jnp.exp(l_end - L)
        rt = r * jnp.exp(L)
        at = a * jnp.exp(L - lw)
        bk4 = cat0(by_head(b * winv), by_head(k * winv))
        bk_end = cat0(b * wend, k * wend)
        pre.append((rt, at, bk4, bk_end, jnp.exp(l_end)))
    tick()

    outs = [_dot_nt(cat0(at, rt), bk4) for rt, at, bk4, _, _ in pre]
    tops = [jnp.where(strict4, o[0:C], 0.0) for o in outs]
    bots = [jnp.where(incl4, o[C:2 * C], 0.0) for o in outs]
    tick()

    aak_v = [_dot(top[:, 2 * C:], by_head(ch[3])) for top, ch in zip(tops, chunks)]
    tick()
    A = [top[:, :2 * C] for top in tops]
    Z = [cat1(p[1], av) for p, av in zip(pre, aak_v)]
    n_steps = int(math.log2(C))
    for s in range(n_steps):
        Z = [z + _dot(a_, by_head(z)) for a_, z in zip(A, Z)]
        tick()
        if s + 1 < n_steps:
            A = [_dot(a_, by_head(a_)) for a_ in A]
            tick()
    ahat = [z[:, :LANES] for z in Z]
    vhat = [z[:, LANES:] for z in Z]

    zc2 = jnp.zeros((2 * C, LANES), F32)
    yr2 = [_dot(bot, cat0(by_head(cat1(vh, ah)), cat1(by_head(ch[3]), zc2)))
           for bot, vh, ah, ch in zip(bots, vhat, ahat, chunks)]
    tick()
    qn = [_dot_tn(cat0(cat1(ah, vh), cat1(zc, ch[3])), p[3])
          for ah, vh, ch, p in zip(ahat, vhat, chunks, pre)]
    tick()
    res = []
    for i in range(n):
        yhat = yr2[i][:, :LANES]
        rhat = pre[i][0] + yr2[i][:, LANES:]
        Q = jnp.where(bd, qn[i][0:LANES], 0.0)
        Nt = jnp.where(bd, qn[i][LANES:], 0.0)
        res.append((rhat, yhat, Q, Nt, pre[i][4]))
    return res


SCAN_LOCAL_STAGES = 5 + 2 * int(math.log2(SCAN_CHUNK)) - 1


def _rwkv_scan_kernel(r_ref, lw_ref, k_ref, v_ref, a_ref, b_ref, g_ref, bg_ref, lnw_ref, lnb_ref,
                      o_ref, s_ref, rh_ref, yh_ref, q_ref, n_ref, wc_ref, *, blocks_per_seq):
    C = SCAN_CHUNK
    n_chunks = r_ref.shape[1] // C
    i = pl.program_id(0)
    n_blocks = pl.num_programs(0) - 1
    prev_starts_seq = jnp.maximum(i - 1, 0) % blocks_per_seq == 0

    @pl.when(i == 0)
    def _():
        s_ref[...] = jnp.zeros(s_ref.shape, F32)
        rh_ref[...] = jnp.zeros(rh_ref.shape, rh_ref.dtype)
        yh_ref[...] = jnp.zeros(yh_ref.shape, F32)
        q_ref[...] = jnp.zeros(q_ref.shape, q_ref.dtype)
        n_ref[...] = jnp.zeros(n_ref.shape, F32)
        wc_ref[...] = jnp.zeros(wc_ref.shape, F32)

    ri = lax.broadcasted_iota(jnp.int32, (C, 4 * C), 0)
    ci = lax.broadcasted_iota(jnp.int32, (C, 4 * C), 1) % C
    strict4 = ci < ri
    incl4 = ci <= ri
    lane = lax.broadcasted_iota(jnp.int32, (1, LANES), 1)
    lane_lo = lane < RWKV_HEAD_DIM
    lane_hi = lane >= RWKV_HEAD_DIM
    r128 = lax.broadcasted_iota(jnp.int32, (LANES, LANES), 0)
    c128 = lax.broadcasted_iota(jnp.int32, (LANES, LANES), 1)
    bd = (r128 < RWKV_HEAD_DIM) == (c128 < RWKV_HEAD_DIM)
    tril = (lax.broadcasted_iota(jnp.int32, (C, C), 1) <= lax.broadcasted_iota(jnp.int32, (C, C), 0)).astype(BF16)
    consts = (tril, strict4, incl4, bd)
    inv_n = 1.0 / RWKV_HEAD_DIM

    def head_sums(x):
        lo = jnp.sum(jnp.where(lane_lo, x, 0.0), axis=-1, keepdims=True)
        hi = jnp.sum(jnp.where(lane_hi, x, 0.0), axis=-1, keepdims=True)
        return jnp.where(lane_lo, lo, hi)

    def recurrence_step(c, S):
        sl = slice(c * C, (c + 1) * C)
        y = _dot_nt(rh_ref[c], S) + yh_ref[c]
        S = S * wc_ref[c] + _dot(S, q_ref[c]) + n_ref[c]
        mean = head_sums(y) * inv_n
        d = y - mean
        var = head_sums(d * d) * inv_n
        yn = d * lax.rsqrt(var + GN_EPS) * lnw_ref[...] + lnb_ref[...]
        o_ref[0, sl, :] = (yn * g_ref[0, sl, :].astype(F32) + bg_ref[0, sl, :].astype(F32)).astype(o_ref.dtype)
        return S

    @pl.when(i < n_blocks)
    def _():
        state = [jnp.where(prev_starts_seq, 0.0, s_ref[...])]
        done = [0]
        ticks = [0]

        def tick():
            ticks[0] += 1
            target = min(n_chunks, (ticks[0] * n_chunks + SCAN_LOCAL_STAGES - 1) // SCAN_LOCAL_STAGES)
            while done[0] < target:
                state[0] = recurrence_step(done[0], state[0])
                done[0] += 1

        chunks = []
        for c in range(n_chunks):
            sl = slice(c * C, (c + 1) * C)
            chunks.append(tuple(ref[0, sl, :].astype(F32)
                                for ref in (r_ref, lw_ref, k_ref, v_ref, a_ref, b_ref)))
        local = _scan_local(chunks, consts, tick)
        assert done[0] == n_chunks
        s_ref[...] = state[0]
        for c in range(n_chunks):
            rhat, yhat, Q, Nt, wc = local[c]
            rh_ref[c] = rhat.astype(rh_ref.dtype)
            yh_ref[c] = yhat
            q_ref[c] = Q.astype(q_ref.dtype)
            n_ref[c] = Nt
            wc_ref[c] = wc

    @pl.when(i == n_blocks)
    def _():
        S = jnp.where(prev_starts_seq, 0.0, s_ref[...])
        for c in range(n_chunks):
            S = recurrence_step(c, S)


def _rwkv_scan(r, lw, k, v, a, b, g, bg, lnw, lnb, tm):
    B, S, W = r.shape
    n_pairs = W // LANES
    n_blocks = S // tm
    n_chunks = tm // SCAN_CHUNK
    total = B * n_pairs * n_blocks

    def block_of(j):
        return j // (n_pairs * n_blocks), j % n_blocks, (j // n_blocks) % n_pairs

    cur = pl.BlockSpec((1, tm, LANES), lambda i: block_of(jnp.minimum(i, total - 1)))
    prev = pl.BlockSpec((1, tm, LANES), lambda i: block_of(jnp.maximum(i - 1, 0)))
    vec = pl.BlockSpec((1, LANES), lambda i: (0, block_of(jnp.maximum(i - 1, 0))[2]))
    return pl.pallas_call(
        functools.partial(_rwkv_scan_kernel, blocks_per_seq=n_blocks),
        grid=(total + 1,),
        in_specs=[cur] * 6 + [prev, prev, vec, vec],
        out_specs=prev,
        out_shape=jax.ShapeDtypeStruct((B, S, W), BF16),
        scratch_shapes=[pltpu.VMEM((LANES, LANES), F32),
                        pltpu.VMEM((n_chunks, SCAN_CHUNK, LANES), BF16),
                        pltpu.VMEM((n_chunks, SCAN_CHUNK, LANES), F32),
                        pltpu.VMEM((n_chunks, LANES, LANES), BF16),
                        pltpu.VMEM((n_chunks, LANES, LANES), F32),
                        pltpu.VMEM((n_chunks, 1, LANES), F32)],
        compiler_params=_cparams("arbitrary"),
        name="rwkv_scan",
    )(r, lw, k, v, a, b, g, bg, lnw.reshape(1, W).astype(F32), lnb.reshape(1, W).astype(F32))


def _post_kernel(x_ref, oa_ref, yr_ref, gates_ref, p_ref, woa_ref, wor_ref, wout_ref, nffn_ref,
                 wg_ref, wu_ref, wd_ref, nple_ref, wple_ref, wpg_ref, nfin_ref, o_ref, *, final, ff_chunk):
    D = x_ref.shape[1]
    ya = jnp.dot(oa_ref[...], woa_ref[...], preferred_element_type=F32)
    yr = jnp.dot(yr_ref[...], wor_ref[...], preferred_element_type=F32)
    m = gates_ref[:, 0:D].astype(F32) * ya + gates_ref[:, D:2 * D].astype(F32) * yr
    x = x_ref[...] + _dot(m, wout_ref[...])

    h2 = _rms(x, nffn_ref[...], NORM_EPS).astype(BF16)
    d_ff = wg_ref.shape[1]
    acc = jnp.zeros_like(x)
    for c0 in range(0, d_ff, ff_chunk):
        gt = jnp.dot(h2, wg_ref[:, c0:c0 + ff_chunk], preferred_element_type=F32)
        up = jnp.dot(h2, wu_ref[:, c0:c0 + ff_chunk], preferred_element_type=F32)
        act = gt * jax.nn.sigmoid(gt) * up
        acc = acc + jnp.dot(act.astype(BF16), wd_ref[c0:c0 + ff_chunk, :], preferred_element_type=F32)
    x = x + acc

    e = _dot(p_ref[...], wple_ref[...])
    gp = jax.nn.sigmoid(_dot(_rms(x, nple_ref[...], NORM_EPS), wpg_ref[...]))
    x = x + gp * e
    if final:
        x = _rms(x, nfin_ref[...], NORM_EPS)
    o_ref[...] = x


def _post(x2d, oa, yr, gates, p3d, layer, woa, wor, wout, nffn, wg, wu, wd, nple, wple, wpg, nfin, final, tm):
    T, D = x2d.shape
    d_ff = wg.shape[1]
    ff_chunk = 256 if d_ff % 256 == 0 else LANES
    tok = lambda w: pl.BlockSpec((tm, w), lambda i: (i, 0))
    p_spec = pl.BlockSpec((None, tm, p3d.shape[2]), lambda i: (layer, i, 0))
    row = lambda v: v.reshape(1, -1).astype(F32)
    bf = lambda w: w.astype(BF16)
    return pl.pallas_call(
        functools.partial(_post_kernel, final=final, ff_chunk=ff_chunk),
        grid=(T // tm,),
        in_specs=[tok(D), tok(oa.shape[1]), tok(yr.shape[1]), tok(gates.shape[1]), p_spec,
                  _const_spec(woa.shape), _const_spec(wor.shape), _const_spec(wout.shape),
                  _const_spec((1, D)), _const_spec(wg.shape), _const_spec(wu.shape), _const_spec(wd.shape),
                  _const_spec((1, D)), _const_spec(wple.shape), _const_spec(wpg.shape), _const_spec((1, D))],
        out_specs=tok(D),
        out_shape=jax.ShapeDtypeStruct((T, D), F32),
        compiler_params=_cparams("parallel"),
        name="post",
    )(x2d, oa, yr, gates, p3d, bf(woa), bf(wor), bf(wout), row(nffn), bf(wg), bf(wu), bf(wd),
      row(nple), bf(wple), bf(wpg), row(nfin))


def _tile(n, pref):
    t = min(n, pref)
    assert n % t == 0, (n, t)
    return t


def kernel(x, p, rel_bias, norm_mix, w_in, lam_q1, lam_k1, lam_q2, lam_k2, attn_subln, rwkv_mu, rwkv_w0, rwkv_w2, rwkv_a0, rwkv_a2, rwkv_g2, rwkv_kk, rwkv_ka, rwkv_rk, rwkv_lnx_w, rwkv_lnx_b, w_out_attn, w_out_rwkv, w_out, norm_ffn, w_ffn_gate, w_ffn_up, w_ffn_down, norm_ple, w_ple, w_ple_gate, norm_final):
    B, S, D = x.shape
    depth = w_in.shape[0]
    T = B * S
    t_attn = _tile(S, ATTN_TILE)
    tr_attn = t_attn // ATTN_ROW_SPLIT
    assert tr_attn % LANES == 0
    tm_in = _tile(T, 1024)
    tm_tok = _tile(T, 512)
    tm_prep = _tile(S, 1024)
    tm_scan = _tile(S, 16 * SCAN_CHUNK)
    assert tm_scan % SCAN_CHUNK == 0

    bias_tiles = _bias_tiles(rel_bias, tr_attn)
    x2d = x.reshape(T, D)
    for i in range(depth):
        qkv, zr, gates = _inproj(x2d, norm_mix[i], w_in[i].astype(BF16), tm_in)

        lam_init = 0.8 - 0.6 * math.exp(-0.3 * i)
        lam = (jnp.exp(jnp.sum(lam_q1[i] * lam_k1[i])) - jnp.exp(jnp.sum(lam_q2[i] * lam_k2[i])) + lam_init)
        scal = jnp.stack([lam.astype(F32), jnp.asarray(1.0 - lam_init, F32)])
        oa = _attention(qkv.reshape(B, S, -1), bias_tiles, scal, attn_subln[i], t_attn,
                        max(1, min(ATTN_FAR_TILES, S // t_attn)))

        r, lw, k, v, a, b, g, bg = _rwkv_prep(zr.reshape(B, S, -1), rwkv_mu[i], rwkv_w0[i], rwkv_w2[i],
                                              rwkv_a0[i], rwkv_a2[i], rwkv_g2[i], rwkv_kk[i], rwkv_ka[i],
                                              rwkv_rk[i].reshape(-1), tm_prep)
        yr = _rwkv_scan(r, lw, k, v, a, b, g, bg, rwkv_lnx_w[i], rwkv_lnx_b[i], tm_scan)

        x2d = _post(x2d, oa.reshape(T, -1), yr.reshape(T, -1), gates, p.reshape(depth, T, -1), i,
                    w_out_attn[i], w_out_rwkv[i], w_out[i], norm_ffn[i], w_ffn_gate[i], w_ffn_up[i],
                    w_ffn_down[i], norm_ple[i], w_ple[i], w_ple_gate[i], norm_final,
                    final=(i == depth - 1), tm=tm_tok)
    return x2d.reshape(B, S, D)
```
